```python
import math
import jax, jax.numpy as jnp
from jax import lax
import numpy as np

D_MODEL = 1024
BATCH = 4
SEQ = 8192
DEPTH = 4
DEC_BATCH = 32
DEC_SEQ = 64
PAST_LEN = 4096

CHUNK = 64
Q_BLOCK = 128
N_HEADS = 8
HEAD_DIM = 64
V_DIM = 2 * HEAD_DIM
QK_WIDTH = N_HEADS * 2 * HEAD_DIM
V_WIDTH = N_HEADS * V_DIM
D_CONV = D_MODEL
CONV_W = 3
D_FF = 4 * D_MODEL
IN_WIDTH = 2 * QK_WIDTH + V_WIDTH + 3 * D_CONV + 2 * D_MODEL
SPLITS = (QK_WIDTH, 2 * QK_WIDTH, 2 * QK_WIDTH + V_WIDTH,
          2 * QK_WIDTH + V_WIDTH + D_CONV, 2 * QK_WIDTH + V_WIDTH + 2 * D_CONV,
          2 * QK_WIDTH + V_WIDTH + 3 * D_CONV, 2 * QK_WIDTH + V_WIDTH + 3 * D_CONV + D_MODEL)
ALPHA = (2.0 * DEPTH) ** 0.25
BETA = (8.0 * DEPTH) ** -0.25
LN_EPS = 1e-5
NEG_INF = -1e30

kernel_name = "streaming_diffattn_shortconv_hybrid_step"


def _layer_norm(x, g=None, b=None):
    x32 = x.astype(jnp.float32)
    mu = jnp.mean(x32, axis=-1, keepdims=True)
    var = jnp.mean(jnp.square(x32 - mu), axis=-1, keepdims=True)
    y = (x32 - mu) * lax.rsqrt(var + LN_EPS)
    if g is not None:
        y = y * g.astype(jnp.float32) + b.astype(jnp.float32)
    return y.astype(x.dtype)


def _lambda_init(layer):
    return 0.8 - 0.6 * math.exp(-0.3 * layer)


def _diff_attend(q, k, v, q_pos, k_pos, lam):
    s = jnp.einsum('bqhnd,bkhnd->bnhqk', q, k).astype(jnp.float32) * (HEAD_DIM ** -0.5)
    visible = (k_pos[None, :] // CHUNK) <= (q_pos[:, None] // CHUNK)
    s = jnp.where(visible[None, None, None], s, NEG_INF)
    p = jax.nn.softmax(s, axis=-1)
    a = p[:, 0] - lam * p[:, 1]
    return jnp.einsum('bhqk,bkhd->bqhd', a.astype(v.dtype), v)


def _prompt_attention(q, k, v, lam):
    b, s = q.shape[0], q.shape[1]
    k_pos = jnp.arange(s, dtype=jnp.int32)

    def one_block(i):
        start = i * Q_BLOCK
        q_blk = lax.dynamic_slice_in_dim(q, start, Q_BLOCK, axis=1)
        q_pos = start + jnp.arange(Q_BLOCK, dtype=jnp.int32)
        return _diff_attend(q_blk, k, v, q_pos, k_pos, lam)

    o = lax.map(one_block, jnp.arange(s // Q_BLOCK, dtype=jnp.int32))
    return jnp.moveaxis(o, 0, 1).reshape(b, s, N_HEADS, V_DIM)


def _token_mixer(u, layer, p, k_past, v_past, conv_past):
    b, t, _ = u.shape
    proj = u @ p['w_in']
    q, k, v, gb_in, gc_in, h_in, g_a, g_b = jnp.split(proj, SPLITS, axis=-1)
    q = q.reshape(b, t, N_HEADS, 2, HEAD_DIM)
    k = k.reshape(b, t, N_HEADS, 2, HEAD_DIM)
    v = v.reshape(b, t, N_HEADS, V_DIM)

    lam_init = _lambda_init(layer)
    lam = (jnp.exp(jnp.sum(p['lam_q1'].astype(jnp.float32) * p['lam_k1'].astype(jnp.float32)))
           - jnp.exp(jnp.sum(p['lam_q2'].astype(jnp.float32) * p['lam_k2'].astype(jnp.float32)))
           + lam_init)

    if k_past is None:
        o = _prompt_attention(q, k, v, lam)
    else:
        past = k_past.shape[1]
        k_all = jnp.concatenate([k_past.astype(k.dtype), k], axis=1)
        v_all = jnp.concatenate([v_past.astype(v.dtype), v], axis=1)
        q_pos = past + jnp.arange(t, dtype=jnp.int32)
        k_pos = jnp.arange(past + t, dtype=jnp.int32)
        o = _diff_attend(q, k_all, v_all, q_pos, k_pos, lam)
    o32 = o.astype(jnp.float32)
    o32 = o32 * lax.rsqrt(jnp.mean(jnp.square(o32), axis=-1, keepdims=True) + LN_EPS)
    o = (o32 * p['subln_g'].astype(jnp.float32) * (1.0 - lam_init)).astype(u.dtype)
    y_a = o.reshape(b, t, V_WIDTH) @ p['w_attn_out']

    z = gc_in * h_in
    if conv_past is None:
        conv_past = jnp.zeros((b, CONV_W - 1, D_CONV), dtype=z.dtype)
    zp = jnp.concatenate([conv_past.astype(z.dtype), z], axis=1)
    cw = p['conv_w']
    conv = cw[0] * zp[:, 0:t]
    for j in range(1, CONV_W):
        conv = conv + cw[j] * zp[:, j:j + t]
    new_conv = zp[:, t:]
    y_b = (gb_in * conv) @ p['w_conv_out']

    mix = (jax.nn.sigmoid(g_a) * y_a + jax.nn.sigmoid(g_b) * y_b) @ p['w_out']
    return mix, k, v, new_conv


def _layer(x, c, layer, p, k_past, v_past, conv_past):
    ada = jax.nn.silu(c) @ p['w_ada'] + p['b_ada']
    shift1, scale1, gate1, shift2, scale2, gate2 = jnp.split(ada, 6, axis=-1)
    u = _layer_norm(x) * (1.0 + scale1[:, None]) + shift1[:, None]
    mix, k_new, v_new, conv_new = _token_mixer(u, layer, p, k_past, v_past, conv_past)
    x = _layer_norm(ALPHA * x + gate1[:, None] * mix, p['ln1_g'], p['ln1_b'])
    u = _layer_norm(x) * (1.0 + scale2[:, None]) + shift2[:, None]
    hid = jnp.square(jax.nn.relu(u @ p['w_ff1'] + p['b_ff1']))
    ff = hid @ p['w_ff2'] + p['b_ff2']
    x = _layer_norm(ALPHA * x + gate2[:, None] * ff, p['ln2_g'], p['ln2_b'])
    return x, k_new, v_new, conv_new


def setup_inputs(seed: int = 0) -> dict:
    key = jax.random.key(seed)
    ks = jax.random.split(key, 32)
    f32 = jnp.float32

    def nrm(k, shape, scale):
        return jax.random.normal(k, shape, dtype=f32) * scale

    col_scale = jnp.concatenate([
        jnp.ones((2 * QK_WIDTH,), f32),
        jnp.full((V_WIDTH,), BETA, f32),
        jnp.ones((2 * D_CONV,), f32),
        jnp.full((D_CONV,), BETA, f32),
        jnp.ones((2 * D_MODEL,), f32)])
    w_in = nrm(ks[7], (DEPTH, D_MODEL, IN_WIDTH), D_MODEL ** -0.5) * col_scale

    b_ada = nrm(ks[18], (DEPTH, 6 * D_MODEL), 0.02)
    gate_offset = jnp.concatenate([jnp.zeros((2 * D_MODEL,), f32), jnp.ones((D_MODEL,), f32),
                                   jnp.zeros((2 * D_MODEL,), f32), jnp.ones((D_MODEL,), f32)])
    b_ada = b_ada + gate_offset

    return {
        'x_prompt': nrm(ks[0], (BATCH, SEQ, D_MODEL), 1.0),
        'x_sample': nrm(ks[1], (DEC_BATCH, DEC_SEQ, D_MODEL), 1.0),
        'c_prompt': nrm(ks[2], (BATCH, D_MODEL), 1.0),
        'c_sample': nrm(ks[3], (DEC_BATCH, D_MODEL), 1.0),
        'cache_k': nrm(ks[4], (DEPTH, DEC_BATCH, PAST_LEN, N_HEADS, 2, HEAD_DIM), 1.0),
        'cache_v': nrm(ks[5], (DEPTH, DEC_BATCH, PAST_LEN, N_HEADS, V_DIM), BETA),
        'state_conv': nrm(ks[6], (DEPTH, DEC_BATCH, CONV_W - 1, D_CONV), 0.5),
        'w_in': w_in,
        'lam_q1': nrm(ks[8], (DEPTH, HEAD_DIM), 0.1),
        'lam_k1': nrm(ks[9], (DEPTH, HEAD_DIM), 0.1),
        'lam_q2': nrm(ks[10], (DEPTH, HEAD_DIM), 0.1),
        'lam_k2': nrm(ks[11], (DEPTH, HEAD_DIM), 0.1),
        'subln_g': 1.0 + nrm(ks[12], (DEPTH, V_DIM), 0.02),
        'w_attn_out': nrm(ks[13], (DEPTH, V_WIDTH, D_MODEL), BETA * V_WIDTH ** -0.5),
        'conv_w': nrm(ks[14], (DEPTH, CONV_W, D_CONV), CONV_W ** -0.5),
        'w_conv_out': nrm(ks[15], (DEPTH, D_CONV, D_MODEL), BETA * D_CONV ** -0.5),
        'w_out': nrm(ks[16], (DEPTH, D_MODEL, D_MODEL), BETA * D_MODEL ** -0.5),
        'w_ada': nrm(ks[17], (DEPTH, D_MODEL, 6 * D_MODEL), 0.1 * D_MODEL ** -0.5),
        'b_ada': b_ada,
        'ln1_g': 1.0 + nrm(ks[19], (DEPTH, D_MODEL), 0.02),
        'ln1_b': nrm(ks[20], (DEPTH, D_MODEL), 0.02),
        'ln2_g': 1.0 + nrm(ks[21], (DEPTH, D_MODEL), 0.02),
        'ln2_b': nrm(ks[22], (DEPTH, D_MODEL), 0.02),
        'w_ff1': nrm(ks[23], (DEPTH, D_MODEL, D_FF), BETA * D_MODEL ** -0.5),
        'b_ff1': nrm(ks[24], (DEPTH, D_FF), 0.02),
        'w_ff2': nrm(ks[25], (DEPTH, D_FF, D_MODEL), BETA * D_FF ** -0.5),
        'b_ff2': nrm(ks[26], (DEPTH, D_MODEL), 0.02),
    }


def reference(x_prompt, x_sample, c_prompt, c_sample, cache_k, cache_v, state_conv,
              w_in, lam_q1, lam_k1, lam_q2, lam_k2, subln_g, w_attn_out, conv_w, w_conv_out,
              w_out, w_ada, b_ada, ln1_g, ln1_b, ln2_g, ln2_b, w_ff1, b_ff1, w_ff2, b_ff2):
    xp, xs = x_prompt, x_sample
    kp_list, vp_list, cp_list = [], [], []
    ks_list, vs_list, cs_list = [], [], []
    for l in range(DEPTH):
        p = {
            'w_in': w_in[l], 'lam_q1': lam_q1[l], 'lam_k1': lam_k1[l], 'lam_q2': lam_q2[l],
            'lam_k2': lam_k2[l], 'subln_g': subln_g[l], 'w_attn_out': w_attn_out[l],
            'conv_w': conv_w[l], 'w_conv_out': w_conv_out[l], 'w_out': w_out[l],
            'w_ada': w_ada[l], 'b_ada': b_ada[l], 'ln1_g': ln1_g[l], 'ln1_b': ln1_b[l],
            'ln2_g': ln2_g[l], 'ln2_b': ln2_b[l], 'w_ff1': w_ff1[l], 'b_ff1': b_ff1[l],
            'w_ff2': w_ff2[l], 'b_ff2': b_ff2[l],
        }
        xp, kp, vp, cp = _layer(xp, c_prompt, l, p, None, None, None)
        xs, ksn, vsn, csn = _layer(xs, c_sample, l, p, cache_k[l], cache_v[l], state_conv[l])
        kp_list.append(kp); vp_list.append(vp); cp_list.append(cp)
        ks_list.append(ksn); vs_list.append(vsn); cs_list.append(csn)
    new_k_prompt = jnp.stack(kp_list, axis=0)
    new_v_prompt = jnp.stack(vp_list, axis=0)
    new_conv_prompt = jnp.stack(cp_list, axis=0)
    new_k_sample = jnp.stack(ks_list, axis=0)
    new_v_sample = jnp.stack(vs_list, axis=0)
    new_conv_sample = jnp.stack(cs_list, axis=0)
    return (xp, xs, new_k_prompt, new_v_prompt, new_conv_prompt, new_k_sample, new_v_sample, new_conv_sample)
```

```python
import functools
import math

import jax
import jax.numpy as jnp
from jax import lax
from jax.experimental import pallas as pl
from jax.experimental.pallas import tpu as pltpu

N_HEADS = 8
HEAD_DIM = 64
V_DIM = 2 * HEAD_DIM
HEAD_W = 2 * HEAD_DIM
CHUNK = 64
CONV_W = 3
LN_EPS = 1e-5
NEG_INF = -1e30
N_SEG = 8

F32 = jnp.float32
BF16 = jnp.bfloat16

VMEM_LIMIT = 56 * 1024 * 1024


def _cparams(n_axes):
    return pltpu.CompilerParams(dimension_semantics=("arbitrary",) * n_axes,
                                vmem_limit_bytes=VMEM_LIMIT)


def _ln(x):
    mu = jnp.mean(x, axis=-1, keepdims=True)
    xc = x - mu
    var = jnp.mean(xc * xc, axis=-1, keepdims=True)
    return xc * lax.rsqrt(var + LN_EPS)


def _dot(a, b):
    return jnp.dot(a, b, preferred_element_type=F32)


def _dot_nt(a, b):
    return lax.dot_general(a, b, (((1,), (1,)), ((), ())), preferred_element_type=F32)


def _ada_kernel(c_ref, w_ref, b_ref, o_ref):
    c = c_ref[...]
    sc = c * jax.nn.sigmoid(c)
    o_ref[...] = _dot(sc.astype(BF16), w_ref[...]) + b_ref[...]


def _ada_call(c_all, w_ada, b_ada):
    n_layers, d, d6 = w_ada.shape
    nb = c_all.shape[0]
    tn = d
    return pl.pallas_call(
        _ada_kernel,
        out_shape=jax.ShapeDtypeStruct((n_layers, nb, d6), F32),
        grid=(n_layers, d6 // tn),
        in_specs=[
            pl.BlockSpec((nb, d), lambda l, j: (0, 0)),
            pl.BlockSpec((None, d, tn), lambda l, j: (l, 0, j)),
            pl.BlockSpec((None, 1, tn), lambda l, j: (l, 0, j)),
        ],
        out_specs=pl.BlockSpec((None, nb, tn), lambda l, j: (l, 0, j)),
        compiler_params=_cparams(2),
        name="ada",
    )(c_all, w_ada, b_ada)


def _proj_kernel(x_ref, ada_ref, wq, wk, wv, wgb, wgc, wh, wga, wgg, cw_ref, past_ref,
                 q_ref, kf_ref, kb_ref, vf_ref, vb_ref, yb_ref, sga_ref, sgb_ref, nc_ref,
                 u_scr, carry_scr, *, bt, ts):
    t = pl.program_id(1)
    j = pl.program_id(2)
    d = x_ref.shape[-1]
    tn = wq.shape[-1]
    m = bt * ts

    @pl.when(j == 0)
    def _():
        xn = _ln(x_ref[...])
        u = xn * (1.0 + ada_ref[:, 1:2, :]) + ada_ref[:, 0:1, :]
        u_scr[...] = u.reshape(m, d).astype(BF16)

    @pl.when(t == 0)
    def _():
        carry_scr[j] = past_ref[...]

    u = u_scr[...]

    def seg(w_ref):
        return _dot(u, w_ref[...])

    q_ref[...] = (seg(wq) * (HEAD_DIM ** -0.5)).astype(BF16).reshape(bt, ts, tn)
    k = seg(wk).reshape(bt, ts, tn)
    kf_ref[...] = k
    kb_ref[...] = k.astype(BF16)
    v = seg(wv).reshape(bt, ts, tn)
    vf_ref[...] = v
    vb_ref[...] = v.astype(BF16)

    z2 = seg(wgc) * seg(wh)
    z = z2.reshape(bt, ts, tn)
    prev = carry_scr[j]
    p0 = prev[:, 0:1, :]
    p1 = prev[:, 1:2, :]
    row = lax.broadcasted_iota(jnp.int32, (bt, ts, tn), 1)
    zr1 = pltpu.roll(z2, 1, 0).reshape(bt, ts, tn)
    zr2 = pltpu.roll(z2, 2, 0).reshape(bt, ts, tn)
    s1 = jnp.where(row == 0, p1, zr1)
    s2 = jnp.where(row == 0, p0, jnp.where(row == 1, p1, zr2))
    cw = cw_ref[...]
    conv = cw[0:1, :].reshape(1, 1, tn) * s2 + cw[1:2, :].reshape(1, 1, tn) * s1
    conv = conv + cw[2:3, :].reshape(1, 1, tn) * z
    tail = z[:, ts - (CONV_W - 1):, :]
    carry_scr[j] = tail
    nc_ref[...] = tail

    yb_ref[...] = (seg(wgb).reshape(bt, ts, tn) * conv).astype(BF16)
    sga_ref[...] = jax.nn.sigmoid(seg(wga)).astype(BF16).reshape(bt, ts, tn)
    sgb_ref[...] = jax.nn.sigmoid(seg(wgg)).astype(BF16).reshape(bt, ts, tn)


def _proj_call(x, ada, w_in, conv_w, conv_past, *, bt, ts, tn, name):
    b, t, d = x.shape
    w = w_in.shape[1] // N_SEG
    nj = w // tn
    grid = (b // bt, t // ts, nj)

    def wspec(s):
        return pl.BlockSpec((d, tn), lambda bi, ti, j, s=s: (0, s * nj + j))

    act = pl.BlockSpec((bt, ts, tn), lambda bi, ti, j: (bi, ti, j))
    tail = pl.BlockSpec((bt, CONV_W - 1, tn), lambda bi, ti, j: (bi, 0, j))
    tails = pl.BlockSpec((bt, None, CONV_W - 1, tn), lambda bi, ti, j: (bi, ti, 0, j))
    sds = jax.ShapeDtypeStruct
    out_shape = (sds((b, t, w), BF16), sds((b, t, w), F32), sds((b, t, w), BF16),
                 sds((b, t, w), F32), sds((b, t, w), BF16), sds((b, t, w), BF16),
                 sds((b, t, w), BF16), sds((b, t, w), BF16), sds((b, t // ts, CONV_W - 1, w), F32))
    return pl.pallas_call(
        functools.partial(_proj_kernel, bt=bt, ts=ts),
        out_shape=out_shape,
        grid=grid,
        in_specs=[
            pl.BlockSpec((bt, ts, d), lambda bi, ti, j: (bi, ti, 0)),
            pl.BlockSpec((bt, 6, d), lambda bi, ti, j: (bi, 0, 0)),
            *[wspec(s) for s in range(N_SEG)],
            pl.BlockSpec((CONV_W, tn), lambda bi, ti, j: (0, j)),
            tail,
        ],
        out_specs=(act,) * 8 + (tails,),
        scratch_shapes=[pltpu.VMEM((bt * ts, d), BF16),
                        pltpu.VMEM((nj, bt, CONV_W - 1, tn), F32)],
        compiler_params=_cparams(3),
        name=name,
    )(x, ada, *([w_in] * N_SEG), conv_w, conv_past)


def _stack_q(q):
    lane = lax.broadcasted_iota(jnp.int32, q.shape, 1)
    zero = jnp.zeros_like(q)
    return jnp.concatenate([jnp.where(lane < HEAD_DIM, q, zero),
                            jnp.where(lane >= HEAD_DIM, q, zero)], axis=0)


def _lam(lq1, lk1, lq2, lk2, lam_init):
    s1 = jnp.sum(lq1[...] * lk1[...], axis=-1, keepdims=True)
    s2 = jnp.sum(lq2[...] * lk2[...], axis=-1, keepdims=True)
    return jnp.exp(s1) - jnp.exp(s2) + lam_init


def _finish_heads(o1, o2, lam, g, lam_init):
    o = o1 - lam * o2
    o = o * lax.rsqrt(jnp.mean(o * o, axis=-1, keepdims=True) + LN_EPS)
    return o * g * (1.0 - lam_init)


def _last_k_tile(qi, tq, tk):
    return ((qi + 1) * tq - 1) // tk


def _pattn_kernel(q_ref, k_ref, v_ref, lq1, lk1, lq2, lk2, g_ref, o_ref,
                  qs_scr, m_scr, l_scr, acc_scr, *, tq, tk, lam_init):
    qi = pl.program_id(2)
    ki = pl.program_id(3)
    last = _last_k_tile(qi, tq, tk)

    @pl.when(ki == 0)
    def _():
        qs_scr[...] = _stack_q(q_ref[...])
        m_scr[...] = jnp.full(m_scr.shape, NEG_INF, F32)
        l_scr[...] = jnp.zeros(l_scr.shape, F32)
        acc_scr[...] = jnp.zeros(acc_scr.shape, F32)

    def update(s):
        m_old = m_scr[...]
        m_new = jnp.maximum(m_old, jnp.max(s, axis=-1, keepdims=True))
        a = jnp.exp(m_old - m_new)
        p = jnp.exp(s - m_new)
        l_scr[...] = a * l_scr[...] + jnp.sum(p, axis=-1, keepdims=True)
        acc_scr[...] = a * acc_scr[...] + _dot(p.astype(BF16), v_ref[...])
        m_scr[...] = m_new

    full = (ki + 1) * tk <= qi * tq + CHUNK

    @pl.when(jnp.logical_and(ki <= last, full))
    def _():
        update(_dot_nt(qs_scr[...], k_ref[...]))

    @pl.when(jnp.logical_and(ki <= last, jnp.logical_not(full)))
    def _():
        s = _dot_nt(qs_scr[...], k_ref[...])
        row = lax.broadcasted_iota(jnp.int32, s.shape, 0)
        row = jnp.where(row >= tq, row - tq, row)
        col = lax.broadcasted_iota(jnp.int32, s.shape, 1)
        visible = (ki * tk + col) // CHUNK <= (qi * tq + row) // CHUNK
        update(jnp.where(visible, s, NEG_INF))

    @pl.when(ki == last)
    def _():
        o = acc_scr[...] / l_scr[...]
        lam = _lam(lq1, lk1, lq2, lk2, lam_init)
        o_ref[...] = _finish_heads(o[:tq], o[tq:], lam, g_ref[...], lam_init).astype(o_ref.dtype)


def _pattn_call(q, k, v, lq1, lk1, lq2, lk2, g, *, tq, tk, lam_init, name):
    b, t, w = q.shape
    nh = w // HEAD_W
    grid = (b, nh, t // tq, t // tk)

    def kv_map(bi, h, qi, ki):
        return (bi, jnp.minimum(ki, _last_k_tile(qi, tq, tk)), h)

    vec = lambda n: pl.BlockSpec((1, n), lambda bi, h, qi, ki: (0, 0))
    return pl.pallas_call(
        functools.partial(_pattn_kernel, tq=tq, tk=tk, lam_init=lam_init),
        out_shape=jax.ShapeDtypeStruct((b, t, w), BF16),
        grid=grid,
        in_specs=[
            pl.BlockSpec((None, tq, HEAD_W), lambda bi, h, qi, ki: (bi, qi, h)),
            pl.BlockSpec((None, tk, HEAD_W), kv_map),
            pl.BlockSpec((None, tk, HEAD_W), kv_map),
            vec(HEAD_DIM), vec(HEAD_DIM), vec(HEAD_DIM), vec(HEAD_DIM), vec(V_DIM),
        ],
        out_specs=pl.BlockSpec((None, tq, HEAD_W), lambda bi, h, qi, ki: (bi, qi, h)),
        scratch_shapes=[pltpu.VMEM((2 * tq, HEAD_W), BF16),
                        pltpu.VMEM((2 * tq, 1), F32),
                        pltpu.VMEM((2 * tq, 1), F32),
                        pltpu.VMEM((2 * tq, V_DIM), F32)],
        compiler_params=_cparams(4),
        name=name,
    )(q, k, v, lq1, lk1, lq2, lk2, g)


def _sattn_kernel(q_ref, kc_ref, vc_ref, kn_ref, vn_ref, lq1, lk1, lq2, lk2, g_ref, o_ref,
                  *, lam_init):
    ts = q_ref.shape[0]
    qs = _stack_q(q_ref[...])
    sc = _dot_nt(qs, kc_ref[...].astype(BF16))
    sn = _dot_nt(qs, kn_ref[...])
    m = jnp.maximum(jnp.max(sc, axis=-1, keepdims=True), jnp.max(sn, axis=-1, keepdims=True))
    pc = jnp.exp(sc - m)
    pn = jnp.exp(sn - m)
    l = jnp.sum(pc, axis=-1, keepdims=True) + jnp.sum(pn, axis=-1, keepdims=True)
    acc = _dot(pc.astype(BF16), vc_ref[...].astype(BF16)) + _dot(pn.astype(BF16), vn_ref[...])
    o = acc / l
    lam = _lam(lq1, lk1, lq2, lk2, lam_init)
    o_ref[...] = _finish_heads(o[:ts], o[ts:], lam, g_ref[...], lam_init).astype(o_ref.dtype)


def _sattn_call(q, cache_k, cache_v, k_new, v_new, lq1, lk1, lq2, lk2, g, *, layer, lam_init, name):
    b, t, w = q.shape
    p = cache_k.shape[2]
    nh = w // HEAD_W
    new = pl.BlockSpec((None, t, HEAD_W), lambda bi, h: (bi, 0, h))
    old = pl.BlockSpec((None, None, p, HEAD_W), lambda bi, h: (layer, bi, 0, h))
    vec = lambda n: pl.BlockSpec((1, n), lambda bi, h: (0, 0))
    return pl.pallas_call(
        functools.partial(_sattn_kernel, lam_init=lam_init),
        out_shape=jax.ShapeDtypeStruct((b, t, w), BF16),
        grid=(b, nh),
        in_specs=[new, old, old, new, new,
                  vec(HEAD_DIM), vec(HEAD_DIM), vec(HEAD_DIM), vec(HEAD_DIM), vec(V_DIM)],
        out_specs=new,
        compiler_params=_cparams(2),
        name=name,
    )(q, cache_k, cache_v, k_new, v_new, lq1, lk1, lq2, lk2, g)


def _mix_kernel(x_ref, ada_ref, on_ref, yb_ref, sga_ref, sgb_ref, wa_ref, wc_ref, wo_ref,
                g_ref, b_ref, out_ref, *, bt, ts, alpha):
    d = x_ref.shape[-1]
    m = bt * ts
    ya = _dot(on_ref[...].reshape(m, d), wa_ref[...])
    yb = _dot(yb_ref[...].reshape(m, d), wc_ref[...])
    merged = (sga_ref[...].reshape(m, d).astype(F32) * ya
              + sgb_ref[...].reshape(m, d).astype(F32) * yb)
    mix = _dot(merged.astype(BF16), wo_ref[...]).reshape(bt, ts, d)
    y = _ln(alpha * x_ref[...] + ada_ref[:, 2:3, :] * mix)
    out_ref[...] = y * g_ref[...].reshape(1, 1, d) + b_ref[...].reshape(1, 1, d)


def _mix_call(x, ada, on, yb, sga, sgb, wa, wc, wo, g, bln, *, bt, ts, alpha, name):
    b, t, d = x.shape
    act = pl.BlockSpec((bt, ts, d), lambda bi, ti: (bi, ti, 0))
    wsp = pl.BlockSpec((d, d), lambda bi, ti: (0, 0))
    vec = pl.BlockSpec((1, d), lambda bi, ti: (0, 0))
    return pl.pallas_call(
        functools.partial(_mix_kernel, bt=bt, ts=ts, alpha=alpha),
        out_shape=jax.ShapeDtypeStruct((b, t, d), F32),
        grid=(b // bt, t // ts),
        in_specs=[act, pl.BlockSpec((bt, 6, d), lambda bi, ti: (bi, 0, 0)),
                  act, act, act, act, wsp, wsp, wsp, vec, vec],
        out_specs=act,
        compiler_params=_cparams(2),
        name=name,
    )(x, ada, on, yb, sga, sgb, wa, wc, wo, g, bln)


def _ffn_kernel(x_ref, ada_ref, w1_ref, b1_ref, w2_ref, b2_ref, g_ref, b_ref, out_ref,
                u_scr, acc_scr, *, bt, ts, alpha):
    f = pl.program_id(2)
    d = x_ref.shape[-1]
    m = bt * ts

    @pl.when(f == 0)
    def _():
        u = _ln(x_ref[...]) * (1.0 + ada_ref[:, 4:5, :]) + ada_ref[:, 3:4, :]
        u_scr[...] = u.reshape(m, d).astype(BF16)
        acc_scr[...] = jnp.zeros(acc_scr.shape, F32)

    hid = jnp.maximum(_dot(u_scr[...], w1_ref[...]) + b1_ref[...], 0.0)
    acc_scr[...] += _dot((hid * hid).astype(BF16), w2_ref[...])

    @pl.when(f == pl.num_programs(2) - 1)
    def _():
        ff = (acc_scr[...] + b2_ref[...]).reshape(bt, ts, d)
        y = _ln(alpha * x_ref[...] + ada_ref[:, 5:6, :] * ff)
        out_ref[...] = y * g_ref[...].reshape(1, 1, d) + b_ref[...].reshape(1, 1, d)


def _ffn_call(x, ada, w1, b1, w2, b2, g, bln, *, bt, ts, tf, alpha, name):
    b, t, d = x.shape
    dff = w1.shape[1]
    act = pl.BlockSpec((bt, ts, d), lambda bi, ti, f: (bi, ti, 0))
    vec = pl.BlockSpec((1, d), lambda bi, ti, f: (0, 0))
    return pl.pallas_call(
        functools.partial(_ffn_kernel, bt=bt, ts=ts, alpha=alpha),
        out_shape=jax.ShapeDtypeStruct((b, t, d), F32),
        grid=(b // bt, t // ts, dff // tf),
        in_specs=[act, pl.BlockSpec((bt, 6, d), lambda bi, ti, f: (bi, 0, 0)),
                  pl.BlockSpec((d, tf), lambda bi, ti, f: (0, f)),
                  pl.BlockSpec((1, tf), lambda bi, ti, f: (0, f)),
                  pl.BlockSpec((tf, d), lambda bi, ti, f: (f, 0)),
                  vec, vec, vec],
        out_specs=act,
        scratch_shapes=[pltpu.VMEM((bt * ts, d), BF16), pltpu.VMEM((bt * ts, d), F32)],
        compiler_params=_cparams(3),
        name=name,
    )(x, ada, w1, b1, w2, b2, g, bln)


def _row_tiles(b, t, rows):
    if t >= rows:
        ts = rows
        while t % ts:
            ts //= 2
        return 1, ts
    bt = max(1, min(b, rows // t))
    while b % bt:
        bt -= 1
    return bt, t


def _div_tile(n, want):
    tile = min(n, want)
    while n % tile:
        tile //= 2
    return tile


def kernel(x_prompt, x_sample, c_prompt, c_sample, cache_k, cache_v, state_conv, w_in, lam_q1, lam_k1, lam_q2, lam_k2, subln_g, w_attn_out, conv_w, w_conv_out, w_out, w_ada, b_ada, ln1_g, ln1_b, ln2_g, ln2_b, w_ff1, b_ff1, w_ff2, b_ff2):
    n_layers, d, _ = w_in.shape
    bp, tp, _ = x_prompt.shape
    bs, tsamp, _ = x_sample.shape
    past = cache_k.shape[2]
    w = N_HEADS * HEAD_W
    alpha = (2.0 * n_layers) ** 0.25

    w_in_b = w_in.astype(BF16)
    w_ao_b = w_attn_out.astype(BF16)
    w_co_b = w_conv_out.astype(BF16)
    w_o_b = w_out.astype(BF16)
    w_ada_b = w_ada.astype(BF16)
    w1_b = w_ff1.astype(BF16)
    w2_b = w_ff2.astype(BF16)

    c_all = jnp.concatenate([c_prompt, c_sample], axis=0)
    ada_all = _ada_call(c_all, w_ada_b, b_ada[:, None, :]).reshape(n_layers, bp + bs, 6, d)

    cache_k2 = cache_k.reshape(n_layers, bs, past, w)
    cache_v2 = cache_v.reshape(n_layers, bs, past, w)
    zero_past = jnp.zeros((bp, CONV_W - 1, d), F32)

    bt_p, ts_p = _row_tiles(bp, tp, 1024)
    bt_s, ts_s = _row_tiles(bs, tsamp, 1024)
    mbt_p, mts_p = _row_tiles(bp, tp, 512)
    mbt_s, mts_s = _row_tiles(bs, tsamp, 512)
    tn = _div_tile(w, 256)
    tf = _div_tile(w_ff1.shape[2], 512)
    tq = _div_tile(tp, 512)
    tk = _div_tile(tp, 512)

    xp, xs = x_prompt, x_sample
    outs = [[] for _ in range(6)]
    for layer in range(n_layers):
        lam_init = 0.8 - 0.6 * math.exp(-0.3 * layer)
        ada_p = ada_all[layer, :bp]
        ada_s = ada_all[layer, bp:]
        lams = (lam_q1[layer][None], lam_k1[layer][None], lam_q2[layer][None], lam_k2[layer][None],
                subln_g[layer][None])
        ln1 = (ln1_g[layer][None], ln1_b[layer][None])
        ffn_w = (w1_b[layer], b_ff1[layer][None], w2_b[layer], b_ff2[layer][None],
                 ln2_g[layer][None], ln2_b[layer][None])

        q, kf, kb, vf, vb, yb, sga, sgb, nc = _proj_call(
            xp, ada_p, w_in_b[layer], conv_w[layer], zero_past,
            bt=bt_p, ts=ts_p, tn=tn, name=f"proj_p{layer}")
        on = _pattn_call(q, kb, vb, *lams, tq=tq, tk=tk, lam_init=lam_init, name=f"attn_p{layer}")
        xp = _mix_call(xp, ada_p, on, yb, sga, sgb, w_ao_b[layer], w_co_b[layer], w_o_b[layer], *ln1,
                       bt=mbt_p, ts=mts_p, alpha=alpha, name=f"mix_p{layer}")
        xp = _ffn_call(xp, ada_p, *ffn_w, bt=bt_p, ts=ts_p, tf=tf, alpha=alpha, name=f"ffn_p{layer}")
        outs[0].append(kf)
        outs[1].append(vf)
        outs[2].append(nc[:, -1])

        q, kf, kb, vf, vb, yb, sga, sgb, nc = _proj_call(
            xs, ada_s, w_in_b[layer], conv_w[layer], state_conv[layer],
            bt=bt_s, ts=ts_s, tn=tn, name=f"proj_s{layer}")
        on = _sattn_call(q, cache_k2, cache_v2, kb, vb, *lams, layer=layer, lam_init=lam_init,
                         name=f"attn_s{layer}")
        xs = _mix_call(xs, ada_s, on, yb, sga, sgb, w_ao_b[layer], w_co_b[layer], w_o_b[layer], *ln1,
                       bt=mbt_s, ts=mts_s, alpha=alpha, name=f"mix_s{layer}")
        xs = _ffn_call(xs, ada_s, *ffn_w, bt=bt_s, ts=ts_s, tf=tf, alpha=alpha, name=f"ffn_s{layer}")
        outs[3].append(kf)
        outs[4].append(vf)
        outs[5].append(nc[:, -1])

    new_k_p = jnp.stack(outs[0]).reshape(n_layers, bp, tp, N_HEADS, 2, HEAD_DIM)
    new_v_p = jnp.stack(outs[1]).reshape(n_layers, bp, tp, N_HEADS, V_DIM)
    new_c_p = jnp.stack(outs[2])
    new_k_s = jnp.stack(outs[3]).reshape(n_layers, bs, tsamp, N_HEADS, 2, HEAD_DIM)
    new_v_s = jnp.stack(outs[4]).reshape(n_layers, bs, tsamp, N_HEADS, V_DIM)
    new_c_s = jnp.stack(outs[5])
    return (xp, xs, new_k_p, new_v_p, new_c_p, new_k_s, new_v_s, new_c_s)
```

```python
import functools
import math

import jax
import jax.numpy as jnp
from jax import lax
from jax.experimental import pallas as pl
from jax.experimental.pallas import tpu as pltpu

N_HEADS = 8
HEAD_DIM = 64
V_DIM = 2 * HEAD_DIM
HEAD_W = 2 * HEAD_DIM
CHUNK = 64
CONV_W = 3
LN_EPS = 1e-5
NEG_INF = -1e30
N_SEG = 8
Q_SCALE = (HEAD_DIM ** -0.5) * math.log2(math.e)
ONES_ROWS = 16

F32 = jnp.float32
BF16 = jnp.bfloat16

VMEM_LIMIT = 56 * 1024 * 1024


def _cparams(n_axes):
    return pltpu.CompilerParams(dimension_semantics=("arbitrary",) * n_axes,
                                vmem_limit_bytes=VMEM_LIMIT)


def _ln(x):
    mu = jnp.mean(x, axis=-1, keepdims=True)
    xc = x - mu
    var = jnp.mean(xc * xc, axis=-1, keepdims=True)
    return xc * lax.rsqrt(var + LN_EPS)


def _dot(a, b):
    return jnp.dot(a, b, preferred_element_type=F32)


def _dot_nt(a, b):
    return lax.dot_general(a, b, (((1,), (1,)), ((), ())), preferred_element_type=F32)


def _ada_kernel(c_ref, w_ref, b_ref, o_ref):
    c = c_ref[...]
    sc = c * jax.nn.sigmoid(c)
    o_ref[...] = _dot(sc.astype(BF16), w_ref[...]) + b_ref[...]


def _ada_call(c_all, w_ada, b_ada):
    n_layers, d, d6 = w_ada.shape
    nb = c_all.shape[0]
    tn = d
    return pl.pallas_call(
        _ada_kernel,
        out_shape=jax.ShapeDtypeStruct((n_layers, nb, d6), F32),
        grid=(n_layers, d6 // tn),
        in_specs=[
            pl.BlockSpec((nb, d), lambda l, j: (0, 0)),
            pl.BlockSpec((None, d, tn), lambda l, j: (l, 0, j)),
            pl.BlockSpec((None, 1, tn), lambda l, j: (l, 0, j)),
        ],
        out_specs=pl.BlockSpec((None, nb, tn), lambda l, j: (l, 0, j)),
        compiler_params=_cparams(2),
        name="ada",
    )(c_all, w_ada, b_ada)


def _proj_kernel(x_ref, ada_ref, wq, wk, wv, wgb, wgc, wh, wga, wgg, cw_ref, past_ref,
                 q_ref, kf_ref, kb_ref, vf_ref, vb_ref, yb_ref, sga_ref, sgb_ref, nc_ref,
                 u_scr, carry_scr, *, bt, ts, feature_major):
    t = pl.program_id(1)
    j = pl.program_id(2)
    d = x_ref.shape[-1]
    tn = wq.shape[-1]
    m = bt * ts

    def attn_layout(a):
        if feature_major:
            return a.T.astype(BF16).reshape(bt, tn, ts)
        return a.astype(BF16).reshape(bt, ts, tn)

    @pl.when(j == 0)
    def _():
        xn = _ln(x_ref[...])
        u = xn * (1.0 + ada_ref[:, 1:2, :]) + ada_ref[:, 0:1, :]
        u_scr[...] = u.reshape(m, d).astype(BF16)

    @pl.when(t == 0)
    def _():
        carry_scr[j] = past_ref[...]

    u = u_scr[...]

    def seg(w_ref):
        return _dot(u, w_ref[...])

    q_ref[...] = attn_layout(seg(wq) * Q_SCALE)
    k = seg(wk).reshape(bt, ts, tn)
    kf_ref[...] = k
    kb_ref[...] = k.astype(BF16)
    v = seg(wv)
    vf_ref[...] = v.reshape(bt, ts, tn)
    vb_ref[...] = attn_layout(v)

    z2 = seg(wgc) * seg(wh)
    z = z2.reshape(bt, ts, tn)
    prev = carry_scr[j]
    p0 = prev[:, 0:1, :]
    p1 = prev[:, 1:2, :]
    row = lax.broadcasted_iota(jnp.int32, (bt, ts, tn), 1)
    zr1 = pltpu.roll(z2, 1, 0).reshape(bt, ts, tn)
    zr2 = pltpu.roll(z2, 2, 0).reshape(bt, ts, tn)
    s1 = jnp.where(row == 0, p1, zr1)
    s2 = jnp.where(row == 0, p0, jnp.where(row == 1, p1, zr2))
    cw = cw_ref[...]
    conv = cw[0:1, :].reshape(1, 1, tn) * s2 + cw[1:2, :].reshape(1, 1, tn) * s1
    conv = conv + cw[2:3, :].reshape(1, 1, tn) * z
    tail = z[:, ts - (CONV_W - 1):, :]
    carry_scr[j] = tail
    nc_ref[...] = tail

    yb_ref[...] = (seg(wgb).reshape(bt, ts, tn) * conv).astype(BF16)
    sga_ref[...] = jax.nn.sigmoid(seg(wga)).astype(BF16).reshape(bt, ts, tn)
    sgb_ref[...] = jax.nn.sigmoid(seg(wgg)).astype(BF16).reshape(bt, ts, tn)


def _proj_call(x, ada, w_in, conv_w, conv_past, *, bt, ts, tn, feature_major, name):
    b, t, d = x.shape
    w = w_in.shape[1] // N_SEG
    nj = w // tn
    grid = (b // bt, t // ts, nj)

    def wspec(s):
        return pl.BlockSpec((d, tn), lambda bi, ti, j, s=s: (0, s * nj + j))

    act = pl.BlockSpec((bt, ts, tn), lambda bi, ti, j: (bi, ti, j))
    sds = jax.ShapeDtypeStruct
    if feature_major:
        attn_act = pl.BlockSpec((bt, tn, ts), lambda bi, ti, j: (bi, j, ti))
        attn_shape = sds((b, w, t), BF16)
    else:
        attn_act, attn_shape = act, sds((b, t, w), BF16)
    tail = pl.BlockSpec((bt, CONV_W - 1, tn), lambda bi, ti, j: (bi, 0, j))
    tails = pl.BlockSpec((bt, None, CONV_W - 1, tn), lambda bi, ti, j: (bi, ti, 0, j))
    out_shape = (attn_shape, sds((b, t, w), F32), sds((b, t, w), BF16),
                 sds((b, t, w), F32), attn_shape, sds((b, t, w), BF16),
                 sds((b, t, w), BF16), sds((b, t, w), BF16), sds((b, t // ts, CONV_W - 1, w), F32))
    return pl.pallas_call(
        functools.partial(_proj_kernel, bt=bt, ts=ts, feature_major=feature_major),
        out_shape=out_shape,
        grid=grid,
        in_specs=[
            pl.BlockSpec((bt, ts, d), lambda bi, ti, j: (bi, ti, 0)),
            pl.BlockSpec((bt, 6, d), lambda bi, ti, j: (bi, 0, 0)),
            *[wspec(s) for s in range(N_SEG)],
            pl.BlockSpec((CONV_W, tn), lambda bi, ti, j: (0, j)),
            tail,
        ],
        out_specs=(attn_act, act, act, act, attn_act, act, act, act, tails),
        scratch_shapes=[pltpu.VMEM((bt * ts, d), BF16),
                        pltpu.VMEM((nj, bt, CONV_W - 1, tn), F32)],
        compiler_params=_cparams(3),
        name=name,
    )(x, ada, *([w_in] * N_SEG), conv_w, conv_past)


def _stack_q(q):
    lane = lax.broadcasted_iota(jnp.int32, q.shape, 1)
    zero = jnp.zeros_like(q)
    return jnp.concatenate([jnp.where(lane < HEAD_DIM, q, zero),
                            jnp.where(lane >= HEAD_DIM, q, zero)], axis=0)


def _lam(lq1, lk1, lq2, lk2, lam_init):
    s1 = jnp.sum(lq1[...] * lk1[...], axis=-1, keepdims=True)
    s2 = jnp.sum(lq2[...] * lk2[...], axis=-1, keepdims=True)
    return jnp.exp(s1) - jnp.exp(s2) + lam_init


def _finish_heads(o1, o2, lam, g, lam_init):
    o = o1 - lam * o2
    o = o * lax.rsqrt(jnp.mean(o * o, axis=-1, keepdims=True) + LN_EPS)
    return o * g * (1.0 - lam_init)


def _pattn_kernel(qt_ref, k_ref, vt_ref, lq1, lk1, lq2, lk2, g_ref, o_ref,
                  qs_scr, acc_scr, sa_scr, sb_scr, *, tq, tk, lam_init):
    qi = pl.program_id(2)
    qt = qt_ref[...]
    feat = lax.broadcasted_iota(jnp.int32, qt.shape, 0)
    zero = jnp.zeros_like(qt)
    qs_scr[:, :tq] = jnp.where(feat < HEAD_DIM, qt, zero)
    qs_scr[:, tq:] = jnp.where(feat >= HEAD_DIM, qt, zero)
    acc_scr[...] = jnp.zeros(acc_scr.shape, F32)
    ones = jnp.ones((ONES_ROWS, tk), BF16)

    def scores(j, dst):
        start = pl.multiple_of(j * tk, tk)
        dst[...] = _dot(k_ref[pl.ds(start, tk), :], qs_scr[...])

    def update(j, src, m_old, masked):
        start = pl.multiple_of(j * tk, tk)
        s = src[...]
        if masked:
            kpos = start + lax.broadcasted_iota(jnp.int32, s.shape, 0)
            col = lax.broadcasted_iota(jnp.int32, s.shape, 1)
            qpos = qi * tq + jnp.where(col >= tq, col - tq, col)
            s = jnp.where(kpos // CHUNK <= qpos // CHUNK, s, NEG_INF)
        m_new = jnp.maximum(m_old, jnp.max(s, axis=0, keepdims=True))
        a = jnp.exp2(m_old - m_new)
        p = jnp.exp2(s - m_new).astype(BF16)
        vt1 = jnp.concatenate([vt_ref[:, pl.ds(start, tk)], ones], axis=0)
        acc_scr[...] = a * acc_scr[...] + _dot(vt1, p)
        return m_new

    m = jnp.full((1, 2 * tq), NEG_INF, F32)
    npair = (qi * tq) // (2 * tk)
    scores(0, sa_scr)

    def body(i, m):
        j = 2 * i
        scores(j + 1, sb_scr)
        m = update(j, sa_scr, m, False)
        scores(j + 2, sa_scr)
        return update(j + 1, sb_scr, m, False)

    m = lax.fori_loop(0, npair, body, m)
    j = 2 * npair
    scores(j + 1, sb_scr)
    m = update(j, sa_scr, m, True)
    update(j + 1, sb_scr, m, True)

    o = acc_scr[:V_DIM, :] / acc_scr[V_DIM:V_DIM + 1, :]
    lam = _lam(lq1, lk1, lq2, lk2, lam_init)
    o = o[:, :tq] - lam * o[:, tq:]
    o = o * lax.rsqrt(jnp.mean(o * o, axis=0, keepdims=True) + LN_EPS)
    o_ref[...] = (o.T * g_ref[...] * (1.0 - lam_init)).astype(o_ref.dtype)


def _pattn_call(qt, k, vt, lq1, lk1, lq2, lk2, g, *, tq, lam_init, name):
    b, t, w = k.shape
    nh = w // HEAD_W
    tk = tq // 2
    vec = lambda n: pl.BlockSpec((1, n), lambda bi, h, qi: (0, 0))
    return pl.pallas_call(
        functools.partial(_pattn_kernel, tq=tq, tk=tk, lam_init=lam_init),
        out_shape=jax.ShapeDtypeStruct((b, t, w), BF16),
        grid=(b, nh, t // tq),
        in_specs=[
            pl.BlockSpec((None, HEAD_W, tq), lambda bi, h, qi: (bi, h, qi)),
            pl.BlockSpec((None, t, HEAD_W), lambda bi, h, qi: (bi, 0, h)),
            pl.BlockSpec((None, HEAD_W, t), lambda bi, h, qi: (bi, h, 0)),
            vec(HEAD_DIM), vec(HEAD_DIM), vec(HEAD_DIM), vec(HEAD_DIM), vec(V_DIM),
        ],
        out_specs=pl.BlockSpec((None, tq, HEAD_W), lambda bi, h, qi: (bi, qi, h)),
        scratch_shapes=[pltpu.VMEM((HEAD_W, 2 * tq), BF16),
                        pltpu.VMEM((V_DIM + ONES_ROWS, 2 * tq), F32),
                        pltpu.VMEM((tk, 2 * tq), F32),
                        pltpu.VMEM((tk, 2 * tq), F32)],
        compiler_params=_cparams(3),
        name=name,
    )(qt, k, vt, lq1, lk1, lq2, lk2, g)


def _sattn_kernel(q_ref, kc_ref, vc_ref, kn_ref, vn_ref, lq1, lk1, lq2, lk2, g_ref, o_ref,
                  *, lam_init):
    ts = q_ref.shape[0]
    qs = _stack_q(q_ref[...])
    sc = _dot_nt(qs, kc_ref[...].astype(BF16))
    sn = _dot_nt(qs, kn_ref[...])
    m = jnp.maximum(jnp.max(sc, axis=-1, keepdims=True), jnp.max(sn, axis=-1, keepdims=True))
    pc = jnp.exp2(sc - m)
    pn = jnp.exp2(sn - m)
    l = jnp.sum(pc, axis=-1, keepdims=True) + jnp.sum(pn, axis=-1, keepdims=True)
    acc = _dot(pc.astype(BF16), vc_ref[...].astype(BF16)) + _dot(pn.astype(BF16), vn_ref[...])
    o = acc / l
    lam = _lam(lq1, lk1, lq2, lk2, lam_init)
    o_ref[...] = _finish_heads(o[:ts], o[ts:], lam, g_ref[...], lam_init).astype(o_ref.dtype)


def _sattn_call(q, cache_k, cache_v, k_new, v_new, lq1, lk1, lq2, lk2, g, *, layer, lam_init, name):
    b, t, w = q.shape
    p = cache_k.shape[2]
    nh = w // HEAD_W
    new = pl.BlockSpec((None, t, HEAD_W), lambda bi, h: (bi, 0, h))
    old = pl.BlockSpec((None, None, p, HEAD_W), lambda bi, h: (layer, bi, 0, h))
    vec = lambda n: pl.BlockSpec((1, n), lambda bi, h: (0, 0))
    return pl.pallas_call(
        functools.partial(_sattn_kernel, lam_init=lam_init),
        out_shape=jax.ShapeDtypeStruct((b, t, w), BF16),
        grid=(b, nh),
        in_specs=[new, old, old, new, new,
                  vec(HEAD_DIM), vec(HEAD_DIM), vec(HEAD_DIM), vec(HEAD_DIM), vec(V_DIM)],
        out_specs=new,
        compiler_params=_cparams(2),
        name=name,
    )(q, cache_k, cache_v, k_new, v_new, lq1, lk1, lq2, lk2, g)


def _mix_kernel(x_ref, ada_ref, on_ref, yb_ref, sga_ref, sgb_ref, wa_ref, wc_ref, wo_ref,
                g_ref, b_ref, out_ref, *, bt, ts, alpha):
    d = x_ref.shape[-1]
    m = bt * ts
    ya = _dot(on_ref[...].reshape(m, d), wa_ref[...])
    yb = _dot(yb_ref[...].reshape(m, d), wc_ref[...])
    merged = (sga_ref[...].reshape(m, d).astype(F32) * ya
              + sgb_ref[...].reshape(m, d).astype(F32) * yb)
    mix = _dot(merged.astype(BF16), wo_ref[...]).reshape(bt, ts, d)
    y = _ln(alpha * x_ref[...] + ada_ref[:, 2:3, :] * mix)
    out_ref[...] = y * g_ref[...].reshape(1, 1, d) + b_ref[...].reshape(1, 1, d)


def _mix_call(x, ada, on, yb, sga, sgb, wa, wc, wo, g, bln, *, bt, ts, alpha, name):
    b, t, d = x.shape
    act = pl.BlockSpec((bt, ts, d), lambda bi, ti: (bi, ti, 0))
    wsp = pl.BlockSpec((d, d), lambda bi, ti: (0, 0))
    vec = pl.BlockSpec((1, d), lambda bi, ti: (0, 0))
    return pl.pallas_call(
        functools.partial(_mix_kernel, bt=bt, ts=ts, alpha=alpha),
        out_shape=jax.ShapeDtypeStruct((b, t, d), F32),
        grid=(b // bt, t // ts),
        in_specs=[act, pl.BlockSpec((bt, 6, d), lambda bi, ti: (bi, 0, 0)),
                  act, act, act, act, wsp, wsp, wsp, vec, vec],
        out_specs=act,
        compiler_params=_cparams(2),
        name=name,
    )(x, ada, on, yb, sga, sgb, wa, wc, wo, g, bln)


def _ffn_kernel(x_ref, ada_ref, w1_ref, b1_ref, w2_ref, b2_ref, g_ref, b_ref, out_ref,
                u_scr, acc_scr, *, bt, ts, alpha):
    f = pl.program_id(2)
    d = x_ref.shape[-1]
    m = bt * ts

    @pl.when(f == 0)
    def _():
        u = _ln(x_ref[...]) * (1.0 + ada_ref[:, 4:5, :]) + ada_ref[:, 3:4, :]
        u_scr[...] = u.reshape(m, d).astype(BF16)
        acc_scr[...] = jnp.zeros(acc_scr.shape, F32)

    hid = jnp.maximum(_dot(u_scr[...], w1_ref[...]) + b1_ref[...], 0.0)
    acc_scr[...] += _dot((hid * hid).astype(BF16), w2_ref[...])

    @pl.when(f == pl.num_programs(2) - 1)
    def _():
        ff = (acc_scr[...] + b2_ref[...]).reshape(bt, ts, d)
        y = _ln(alpha * x_ref[...] + ada_ref[:, 5:6, :] * ff)
        out_ref[...] = y * g_ref[...].reshape(1, 1, d) + b_ref[...].reshape(1, 1, d)


def _ffn_call(x, ada, w1, b1, w2, b2, g, bln, *, bt, ts, tf, alpha, name):
    b, t, d = x.shape
    dff = w1.shape[1]
    act = pl.BlockSpec((bt, ts, d), lambda bi, ti, f: (bi, ti, 0))
    vec = pl.BlockSpec((1, d), lambda bi, ti, f: (0, 0))
    return pl.pallas_call(
        functools.partial(_ffn_kernel, bt=bt, ts=ts, alpha=alpha),
        out_shape=jax.ShapeDtypeStruct((b, t, d), F32),
        grid=(b // bt, t // ts, dff // tf),
        in_specs=[act, pl.BlockSpec((bt, 6, d), lambda bi, ti, f: (bi, 0, 0)),
                  pl.BlockSpec((d, tf), lambda bi, ti, f: (0, f)),
                  pl.BlockSpec((1, tf), lambda bi, ti, f: (0, f)),
                  pl.BlockSpec((tf, d), lambda bi, ti, f: (f, 0)),
                  vec, vec, vec],
        out_specs=act,
        scratch_shapes=[pltpu.VMEM((bt * ts, d), BF16), pltpu.VMEM((bt * ts, d), F32)],
        compiler_params=_cparams(3),
        name=name,
    )(x, ada, w1, b1, w2, b2, g, bln)


def _row_tiles(b, t, rows):
    if t >= rows:
        ts = rows
        while t % ts:
            ts //= 2
        return 1, ts
    bt = max(1, min(b, rows // t))
    while b % bt:
        bt -= 1
    return bt, t


def _div_tile(n, want):
    tile = min(n, want)
    while n % tile:
        tile //= 2
    return tile


def kernel(x_prompt, x_sample, c_prompt, c_sample, cache_k, cache_v, state_conv, w_in, lam_q1, lam_k1, lam_q2, lam_k2, subln_g, w_attn_out, conv_w, w_conv_out, w_out, w_ada, b_ada, ln1_g, ln1_b, ln2_g, ln2_b, w_ff1, b_ff1, w_ff2, b_ff2):
    n_layers, d, _ = w_in.shape
    bp, tp, _ = x_prompt.shape
    bs, tsamp, _ = x_sample.shape
    past = cache_k.shape[2]
    w = N_HEADS * HEAD_W
    alpha = (2.0 * n_layers) ** 0.25

    w_in_b = w_in.astype(BF16)
    w_ao_b = w_attn_out.astype(BF16)
    w_co_b = w_conv_out.astype(BF16)
    w_o_b = w_out.astype(BF16)
    w_ada_b = w_ada.astype(BF16)
    w1_b = w_ff1.astype(BF16)
    w2_b = w_ff2.astype(BF16)

    c_all = jnp.concatenate([c_prompt, c_sample], axis=0)
    ada_all = _ada_call(c_all, w_ada_b, b_ada[:, None, :]).reshape(n_layers, bp + bs, 6, d)

    cache_k2 = cache_k.reshape(n_layers, bs, past, w)
    cache_v2 = cache_v.reshape(n_layers, bs, past, w)
    zero_past = jnp.zeros((bp, CONV_W - 1, d), F32)

    bt_p, ts_p = _row_tiles(bp, tp, 1024)
    bt_s, ts_s = _row_tiles(bs, tsamp, 1024)
    mbt_p, mts_p = _row_tiles(bp, tp, 512)
    mbt_s, mts_s = _row_tiles(bs, tsamp, 512)
    tn = _div_tile(w, 256)
    tf = _div_tile(w_ff1.shape[2], 512)
    tq = _div_tile(tp, 512)

    xp, xs = x_prompt, x_sample
    outs = [[] for _ in range(6)]
    for layer in range(n_layers):
        lam_init = 0.8 - 0.6 * math.exp(-0.3 * layer)
        ada_p = ada_all[layer, :bp]
        ada_s = ada_all[layer, bp:]
        lams = (lam_q1[layer][None], lam_k1[layer][None], lam_q2[layer][None], lam_k2[layer][None],
                subln_g[layer][None])
        ln1 = (ln1_g[layer][None], ln1_b[layer][None])
        ffn_w = (w1_b[layer], b_ff1[layer][None], w2_b[layer], b_ff2[layer][None],
                 ln2_g[layer][None], ln2_b[layer][None])

        q, kf, kb, vf, vb, yb, sga, sgb, nc = _proj_call(
            xp, ada_p, w_in_b[layer], conv_w[layer], zero_past,
            bt=bt_p, ts=ts_p, tn=tn, feature_major=True, name=f"proj_p{layer}")
        on = _pattn_call(q, kb, vb, *lams, tq=tq, lam_init=lam_init, name=f"attn_p{layer}")
        xp = _mix_call(xp, ada_p, on, yb, sga, sgb, w_ao_b[layer], w_co_b[layer], w_o_b[layer], *ln1,
                       bt=mbt_p, ts=mts_p, alpha=alpha, name=f"mix_p{layer}")
        xp = _ffn_call(xp, ada_p, *ffn_w, bt=bt_p, ts=ts_p, tf=tf, alpha=alpha, name=f"ffn_p{layer}")
        outs[0].append(kf)
        outs[1].append(vf)
        outs[2].append(nc[:, -1])

        q, kf, kb, vf, vb, yb, sga, sgb, nc = _proj_call(
            xs, ada_s, w_in_b[layer], conv_w[layer], state_conv[layer],
            bt=bt_s, ts=ts_s, tn=tn, feature_major=False, name=f"proj_s{layer}")
        on = _sattn_call(q, cache_k2, cache_v2, kb, vb, *lams, layer=layer, lam_init=lam_init,
                         name=f"attn_s{layer}")
        xs = _mix_call(xs, ada_s, on, yb, sga, sgb, w_ao_b[layer], w_co_b[layer], w_o_b[layer], *ln1,
                       bt=mbt_s, ts=mts_s, alpha=alpha, name=f"mix_s{layer}")
        xs = _ffn_call(xs, ada_s, *ffn_w, bt=bt_s, ts=ts_s, tf=tf, alpha=alpha, name=f"ffn_s{layer}")
        outs[3].append(kf)
        outs[4].append(vf)
        outs[5].append(nc[:, -1])

    new_k_p = jnp.stack(outs[0]).reshape(n_layers, bp, tp, N_HEADS, 2, HEAD_DIM)
    new_v_p = jnp.stack(outs[1]).reshape(n_layers, bp, tp, N_HEADS, V_DIM)
    new_c_p = jnp.stack(outs[2])
    new_k_s = jnp.stack(outs[3]).reshape(n_layers, bs, tsamp, N_HEADS, 2, HEAD_DIM)
    new_v_s = jnp.stack(outs[4]).reshape(n_layers, bs, tsamp, N_HEADS, V_DIM)
    new_c_s = jnp.stack(outs[5])
    return (xp, xs, new_k_p, new_v_p, new_c_p, new_k_s, new_v_s, new_c_s)
```

```python
import functools
import math

import jax
import jax.numpy as jnp
from jax import lax
from jax.experimental import pallas as pl
from jax.experimental.pallas import tpu as pltpu

N_HEADS = 8
HEAD_DIM = 64
V_DIM = 2 * HEAD_DIM
HEAD_W = 2 * HEAD_DIM
CHUNK = 64
CONV_W = 3
LN_EPS = 1e-5
NEG_INF = -1e30
N_SEG = 8
Q_SCALE = (HEAD_DIM ** -0.5) * math.log2(math.e)
ONES_ROWS = 16

F32 = jnp.float32
BF16 = jnp.bfloat16

VMEM_LIMIT = 56 * 1024 * 1024


def _cparams(n_axes):
    return pltpu.CompilerParams(dimension_semantics=("arbitrary",) * n_axes,
                                vmem_limit_bytes=VMEM_LIMIT)


def _ln(x):
    mu = jnp.mean(x, axis=-1, keepdims=True)
    xc = x - mu
    var = jnp.mean(xc * xc, axis=-1, keepdims=True)
    return xc * lax.rsqrt(var + LN_EPS)


def _dot(a, b):
    return jnp.dot(a, b, preferred_element_type=F32)


def _dot_nt(a, b):
    return lax.dot_general(a, b, (((1,), (1,)), ((), ())), preferred_element_type=F32)


def _ada_kernel(c_ref, w_ref, b_ref, o_ref):
    c = c_ref[...]
    sc = c * jax.nn.sigmoid(c)
    o_ref[...] = _dot(sc.astype(BF16), w_ref[...]) + b_ref[...]


def _ada_call(c_all, w_ada, b_ada):
    n_layers, d, d6 = w_ada.shape
    nb = c_all.shape[0]
    tn = d
    return pl.pallas_call(
        _ada_kernel,
        out_shape=jax.ShapeDtypeStruct((n_layers, nb, d6), F32),
        grid=(n_layers, d6 // tn),
        in_specs=[
            pl.BlockSpec((nb, d), lambda l, j: (0, 0)),
            pl.BlockSpec((None, d, tn), lambda l, j: (l, 0, j)),
            pl.BlockSpec((None, 1, tn), lambda l, j: (l, 0, j)),
        ],
        out_specs=pl.BlockSpec((None, nb, tn), lambda l, j: (l, 0, j)),
        compiler_params=_cparams(2),
        name="ada",
    )(c_all, w_ada, b_ada)


def _proj_kernel(x_ref, ada_ref, wq, wk, wv, wgb, wgc, wh, wga, wgg, cw_ref, past_ref,
                 q_ref, kf_ref, kb_ref, vf_ref, vb_ref, yb_ref, sga_ref, sgb_ref, nc_ref,
                 u_scr, carry_scr, *, bt, ts, feature_major):
    t = pl.program_id(1)
    j = pl.program_id(2)
    d = x_ref.shape[-1]
    tn = wq.shape[-1]
    m = bt * ts

    def attn_layout(a):
        if feature_major:
            return a.T.astype(BF16).reshape(bt, tn, ts)
        return a.astype(BF16).reshape(bt, ts, tn)

    @pl.when(j == 0)
    def _():
        xn = _ln(x_ref[...])
        u = xn * (1.0 + ada_ref[:, 1:2, :]) + ada_ref[:, 0:1, :]
        u_scr[...] = u.reshape(m, d).astype(BF16)

    @pl.when(t == 0)
    def _():
        carry_scr[j] = past_ref[...]

    u = u_scr[...]

    def seg(w_ref):
        return _dot(u, w_ref[...])

    q_ref[...] = attn_layout(seg(wq) * Q_SCALE)
    k = seg(wk).reshape(bt, ts, tn)
    kf_ref[...] = k
    kb_ref[...] = k.astype(BF16)
    v = seg(wv)
    vf_ref[...] = v.reshape(bt, ts, tn)
    vb_ref[...] = attn_layout(v)

    z2 = seg(wgc) * seg(wh)
    z = z2.reshape(bt, ts, tn)
    prev = carry_scr[j]
    p0 = prev[:, 0:1, :]
    p1 = prev[:, 1:2, :]
    row = lax.broadcasted_iota(jnp.int32, (bt, ts, tn), 1)
    zr1 = pltpu.roll(z2, 1, 0).reshape(bt, ts, tn)
    zr2 = pltpu.roll(z2, 2, 0).reshape(bt, ts, tn)
    s1 = jnp.where(row == 0, p1, zr1)
    s2 = jnp.where(row == 0, p0, jnp.where(row == 1, p1, zr2))
    cw = cw_ref[...]
    conv = cw[0:1, :].reshape(1, 1, tn) * s2 + cw[1:2, :].reshape(1, 1, tn) * s1
    conv = conv + cw[2:3, :].reshape(1, 1, tn) * z
    tail = z[:, ts - (CONV_W - 1):, :]
    carry_scr[j] = tail
    nc_ref[...] = tail

    yb_ref[...] = (seg(wgb).reshape(bt, ts, tn) * conv).astype(BF16)
    sga_ref[...] = jax.nn.sigmoid(seg(wga)).astype(BF16).reshape(bt, ts, tn)
    sgb_ref[...] = jax.nn.sigmoid(seg(wgg)).astype(BF16).reshape(bt, ts, tn)


def _proj_call(x, ada, w_in, conv_w, conv_past, *, bt, ts, tn, feature_major, name):
    b, t, d = x.shape
    w = w_in.shape[1] // N_SEG
    nj = w // tn
    grid = (b // bt, t // ts, nj)

    def wspec(s):
        return pl.BlockSpec((d, tn), lambda bi, ti, j, s=s: (0, s * nj + j))

    act = pl.BlockSpec((bt, ts, tn), lambda bi, ti, j: (bi, ti, j))
    sds = jax.ShapeDtypeStruct
    if feature_major:
        attn_act = pl.BlockSpec((bt, tn, ts), lambda bi, ti, j: (bi, j, ti))
        attn_shape = sds((b, w, t), BF16)
    else:
        attn_act, attn_shape = act, sds((b, t, w), BF16)
    tail = pl.BlockSpec((bt, CONV_W - 1, tn), lambda bi, ti, j: (bi, 0, j))
    tails = pl.BlockSpec((bt, None, CONV_W - 1, tn), lambda bi, ti, j: (bi, ti, 0, j))
    out_shape = (attn_shape, sds((b, t, w), F32), sds((b, t, w), BF16),
                 sds((b, t, w), F32), attn_shape, sds((b, t, w), BF16),
                 sds((b, t, w), BF16), sds((b, t, w), BF16), sds((b, t // ts, CONV_W - 1, w), F32))
    return pl.pallas_call(
        functools.partial(_proj_kernel, bt=bt, ts=ts, feature_major=feature_major),
        out_shape=out_shape,
        grid=grid,
        in_specs=[
            pl.BlockSpec((bt, ts, d), lambda bi, ti, j: (bi, ti, 0)),
            pl.BlockSpec((bt, 6, d), lambda bi, ti, j: (bi, 0, 0)),
            *[wspec(s) for s in range(N_SEG)],
            pl.BlockSpec((CONV_W, tn), lambda bi, ti, j: (0, j)),
            tail,
        ],
        out_specs=(attn_act, act, act, act, attn_act, act, act, act, tails),
        scratch_shapes=[pltpu.VMEM((bt * ts, d), BF16),
                        pltpu.VMEM((nj, bt, CONV_W - 1, tn), F32)],
        compiler_params=_cparams(3),
        name=name,
    )(x, ada, *([w_in] * N_SEG), conv_w, conv_past)


def _stack_q(q):
    lane = lax.broadcasted_iota(jnp.int32, q.shape, 1)
    zero = jnp.zeros_like(q)
    return jnp.concatenate([jnp.where(lane < HEAD_DIM, q, zero),
                            jnp.where(lane >= HEAD_DIM, q, zero)], axis=0)


def _lam(lq1, lk1, lq2, lk2, lam_init):
    s1 = jnp.sum(lq1[...] * lk1[...], axis=-1, keepdims=True)
    s2 = jnp.sum(lq2[...] * lk2[...], axis=-1, keepdims=True)
    return jnp.exp(s1) - jnp.exp(s2) + lam_init


def _finish_heads(o1, o2, lam, g, lam_init):
    o = o1 - lam * o2
    o = o * lax.rsqrt(jnp.mean(o * o, axis=-1, keepdims=True) + LN_EPS)
    return o * g * (1.0 - lam_init)


def _pattn_kernel(qt_ref, k_ref, vt_ref, lq1, lk1, lq2, lk2, g_ref, o_ref,
                  qs_scr, acc_scr, sa_scr, sb_scr, *, tq, tk, lam_init):
    qi = pl.program_id(2)
    qt = qt_ref[...]
    feat = lax.broadcasted_iota(jnp.int32, qt.shape, 0)
    zero = jnp.zeros_like(qt)
    qs_scr[:, :tq] = jnp.where(feat < HEAD_DIM, qt, zero)
    qs_scr[:, tq:] = jnp.where(feat >= HEAD_DIM, qt, zero)
    acc_scr[...] = jnp.zeros(acc_scr.shape, F32)
    ones = jnp.ones((ONES_ROWS, tk), BF16)

    def scores(j, dst):
        start = pl.multiple_of(j * tk, tk)
        dst[...] = _dot(k_ref[pl.ds(start, tk), :], qs_scr[...])

    def update(j, src, m_old, masked):
        start = pl.multiple_of(j * tk, tk)
        s = src[...]
        if masked:
            kpos = start + lax.broadcasted_iota(jnp.int32, s.shape, 0)
            col = lax.broadcasted_iota(jnp.int32, s.shape, 1)
            qpos = qi * tq + jnp.where(col >= tq, col - tq, col)
            s = jnp.where(kpos // CHUNK <= qpos // CHUNK, s, NEG_INF)
        m_new = jnp.maximum(m_old, jnp.max(s, axis=0, keepdims=True))
        a = jnp.exp2(m_old - m_new)
        p = jnp.exp2(s - m_new).astype(BF16)
        vt1 = jnp.concatenate([vt_ref[:, pl.ds(start, tk)], ones], axis=0)
        acc_scr[...] = a * acc_scr[...] + _dot(vt1, p)
        return m_new

    m = jnp.full((1, 2 * tq), NEG_INF, F32)
    npair = (qi * tq) // (2 * tk)
    scores(0, sa_scr)

    def body(i, m):
        j = 2 * i
        scores(j + 1, sb_scr)
        m = update(j, sa_scr, m, False)
        scores(j + 2, sa_scr)
        return update(j + 1, sb_scr, m, False)

    m = lax.fori_loop(0, npair, body, m)
    j = 2 * npair
    scores(j + 1, sb_scr)
    m = update(j, sa_scr, m, True)
    update(j + 1, sb_scr, m, True)

    o = acc_scr[:V_DIM, :] / acc_scr[V_DIM:V_DIM + 1, :]
    lam = _lam(lq1, lk1, lq2, lk2, lam_init)
    o = o[:, :tq] - lam * o[:, tq:]
    o = o * lax.rsqrt(jnp.mean(o * o, axis=0, keepdims=True) + LN_EPS)
    o_ref[...] = (o.T * g_ref[...] * (1.0 - lam_init)).astype(o_ref.dtype)


def _pattn_call(qt, k, vt, lq1, lk1, lq2, lk2, g, *, tq, lam_init, name):
    b, t, w = k.shape
    nh = w // HEAD_W
    tk = tq // 2
    vec = lambda n: pl.BlockSpec((1, n), lambda bi, h, qi: (0, 0))
    return pl.pallas_call(
        functools.partial(_pattn_kernel, tq=tq, tk=tk, lam_init=lam_init),
        out_shape=jax.ShapeDtypeStruct((b, t, w), BF16),
        grid=(b, nh, t // tq),
        in_specs=[
            pl.BlockSpec((None, HEAD_W, tq), lambda bi, h, qi: (bi, h, qi)),
            pl.BlockSpec((None, t, HEAD_W), lambda bi, h, qi: (bi, 0, h)),
            pl.BlockSpec((None, HEAD_W, t), lambda bi, h, qi: (bi, h, 0)),
            vec(HEAD_DIM), vec(HEAD_DIM), vec(HEAD_DIM), vec(HEAD_DIM), vec(V_DIM),
        ],
        out_specs=pl.BlockSpec((None, tq, HEAD_W), lambda bi, h, qi: (bi, qi, h)),
        scratch_shapes=[pltpu.VMEM((HEAD_W, 2 * tq), BF16),
                        pltpu.VMEM((V_DIM + ONES_ROWS, 2 * tq), F32),
                        pltpu.VMEM((tk, 2 * tq), F32),
                        pltpu.VMEM((tk, 2 * tq), F32)],
        compiler_params=_cparams(3),
        name=name,
    )(qt, k, vt, lq1, lk1, lq2, lk2, g)


def _sattn_kernel(q_ref, kt_ref, vc_ref, kn_ref, vn_ref, lq1, lk1, lq2, lk2, g_ref, o_ref,
                  qs_scr, m_scr, l_scr, acc_scr, *, lam_init):
    c = pl.program_id(1)
    ts = q_ref.shape[0]
    nh = acc_scr.shape[0]

    def hs(h):
        return slice(h * HEAD_W, (h + 1) * HEAD_W)

    @pl.when(c == 0)
    def _():
        for h in range(nh):
            qs = _stack_q(q_ref[:, hs(h)])
            qs_scr[h] = qs
            s = _dot_nt(qs, kn_ref[:, hs(h)])
            m = jnp.max(s, axis=-1, keepdims=True)
            p = jnp.exp2(s - m)
            m_scr[h] = m
            l_scr[h] = jnp.sum(p, axis=-1, keepdims=True)
            acc_scr[h] = _dot(p.astype(BF16), vn_ref[:, hs(h)])

    vh = pltpu.einshape("phd->hpd", vc_ref[...]).astype(BF16)
    for h in range(nh):
        s = _dot(qs_scr[h], kt_ref[hs(h), :].astype(BF16))
        m_old = m_scr[h]
        m_new = jnp.maximum(m_old, jnp.max(s, axis=-1, keepdims=True))
        a = jnp.exp2(m_old - m_new)
        p = jnp.exp2(s - m_new)
        m_scr[h] = m_new
        l_scr[h] = a * l_scr[h] + jnp.sum(p, axis=-1, keepdims=True)
        acc_scr[h] = a * acc_scr[h] + _dot(p.astype(BF16), vh[h])

    @pl.when(c == pl.num_programs(1) - 1)
    def _():
        lam = _lam(lq1, lk1, lq2, lk2, lam_init)
        for h in range(nh):
            o = acc_scr[h] / l_scr[h]
            o_ref[:, hs(h)] = _finish_heads(o[:ts], o[ts:], lam, g_ref[...], lam_init).astype(o_ref.dtype)


def _sattn_call(q, cache_kt, cache_v, k_new, v_new, lq1, lk1, lq2, lk2, g, *, layer, pc, lam_init, name):
    b, t, w = q.shape
    p = cache_kt.shape[3]
    nh = w // HEAD_W
    new = pl.BlockSpec((None, t, w), lambda bi, c: (bi, 0, 0))
    vec = lambda n: pl.BlockSpec((1, n), lambda bi, c: (0, 0))
    return pl.pallas_call(
        functools.partial(_sattn_kernel, lam_init=lam_init),
        out_shape=jax.ShapeDtypeStruct((b, t, w), BF16),
        grid=(b, p // pc),
        in_specs=[new,
                  pl.BlockSpec((None, None, w, pc), lambda bi, c: (layer, bi, 0, c)),
                  pl.BlockSpec((None, None, pc, nh, V_DIM), lambda bi, c: (layer, bi, c, 0, 0)),
                  new, new,
                  vec(HEAD_DIM), vec(HEAD_DIM), vec(HEAD_DIM), vec(HEAD_DIM), vec(V_DIM)],
        out_specs=new,
        scratch_shapes=[pltpu.VMEM((nh, 2 * t, HEAD_W), BF16),
                        pltpu.VMEM((nh, 2 * t, 1), F32),
                        pltpu.VMEM((nh, 2 * t, 1), F32),
                        pltpu.VMEM((nh, 2 * t, V_DIM), F32)],
        compiler_params=_cparams(2),
        name=name,
    )(q, cache_kt, cache_v, k_new, v_new, lq1, lk1, lq2, lk2, g)


def _mix_kernel(x_ref, ada_ref, on_ref, yb_ref, sga_ref, sgb_ref, wa_ref, wc_ref, wo_ref,
                g_ref, b_ref, out_ref, *, bt, ts, alpha):
    d = x_ref.shape[-1]
    m = bt * ts
    ya = _dot(on_ref[...].reshape(m, d), wa_ref[...])
    yb = _dot(yb_ref[...].reshape(m, d), wc_ref[...])
    merged = (sga_ref[...].reshape(m, d).astype(F32) * ya
              + sgb_ref[...].reshape(m, d).astype(F32) * yb)
    mix = _dot(merged.astype(BF16), wo_ref[...]).reshape(bt, ts, d)
    y = _ln(alpha * x_ref[...] + ada_ref[:, 2:3, :] * mix)
    out_ref[...] = y * g_ref[...].reshape(1, 1, d) + b_ref[...].reshape(1, 1, d)


def _mix_call(x, ada, on, yb, sga, sgb, wa, wc, wo, g, bln, *, bt, ts, alpha, name):
    b, t, d = x.shape
    act = pl.BlockSpec((bt, ts, d), lambda bi, ti: (bi, ti, 0))
    wsp = pl.BlockSpec((d, d), lambda bi, ti: (0, 0))
    vec = pl.BlockSpec((1, d), lambda bi, ti: (0, 0))
    return pl.pallas_call(
        functools.partial(_mix_kernel, bt=bt, ts=ts, alpha=alpha),
        out_shape=jax.ShapeDtypeStruct((b, t, d), F32),
        grid=(b // bt, t // ts),
        in_specs=[act, pl.BlockSpec((bt, 6, d), lambda bi, ti: (bi, 0, 0)),
                  act, act, act, act, wsp, wsp, wsp, vec, vec],
        out_specs=act,
        compiler_params=_cparams(2),
        name=name,
    )(x, ada, on, yb, sga, sgb, wa, wc, wo, g, bln)


def _ffn_kernel(x_ref, ada_ref, w1_ref, b1_ref, w2_ref, b2_ref, g_ref, b_ref, out_ref,
                u_scr, acc_scr, *, bt, ts, alpha):
    f = pl.program_id(2)
    d = x_ref.shape[-1]
    m = bt * ts

    @pl.when(f == 0)
    def _():
        u = _ln(x_ref[...]) * (1.0 + ada_ref[:, 4:5, :]) + ada_ref[:, 3:4, :]
        u_scr[...] = u.reshape(m, d).astype(BF16)
        acc_scr[...] = jnp.zeros(acc_scr.shape, F32)

    hid = jnp.maximum(_dot(u_scr[...], w1_ref[...]) + b1_ref[...], 0.0)
    acc_scr[...] += _dot((hid * hid).astype(BF16), w2_ref[...])

    @pl.when(f == pl.num_programs(2) - 1)
    def _():
        ff = (acc_scr[...] + b2_ref[...]).reshape(bt, ts, d)
        y = _ln(alpha * x_ref[...] + ada_ref[:, 5:6, :] * ff)
        out_ref[...] = y * g_ref[...].reshape(1, 1, d) + b_ref[...].reshape(1, 1, d)


def _ffn_call(x, ada, w1, b1, w2, b2, g, bln, *, bt, ts, tf, alpha, name):
    b, t, d = x.shape
    dff = w1.shape[1]
    act = pl.BlockSpec((bt, ts, d), lambda bi, ti, f: (bi, ti, 0))
    vec = pl.BlockSpec((1, d), lambda bi, ti, f: (0, 0))
    return pl.pallas_call(
        functools.partial(_ffn_kernel, bt=bt, ts=ts, alpha=alpha),
        out_shape=jax.ShapeDtypeStruct((b, t, d), F32),
        grid=(b // bt, t // ts, dff // tf),
        in_specs=[act, pl.BlockSpec((bt, 6, d), lambda bi, ti, f: (bi, 0, 0)),
                  pl.BlockSpec((d, tf), lambda bi, ti, f: (0, f)),
                  pl.BlockSpec((1, tf), lambda bi, ti, f: (0, f)),
                  pl.BlockSpec((tf, d), lambda bi, ti, f: (f, 0)),
                  vec, vec, vec],
        out_specs=act,
        scratch_shapes=[pltpu.VMEM((bt * ts, d), BF16), pltpu.VMEM((bt * ts, d), F32)],
        compiler_params=_cparams(3),
        name=name,
    )(x, ada, w1, b1, w2, b2, g, bln)


def _row_tiles(b, t, rows):
    if t >= rows:
        ts = rows
        while t % ts:
            ts //= 2
        return 1, ts
    bt = max(1, min(b, rows // t))
    while b % bt:
        bt -= 1
    return bt, t


def _div_tile(n, want):
    tile = min(n, want)
    while n % tile:
        tile //= 2
    return tile


def kernel(x_prompt, x_sample, c_prompt, c_sample, cache_k, cache_v, state_conv, w_in, lam_q1, lam_k1, lam_q2, lam_k2, subln_g, w_attn_out, conv_w, w_conv_out, w_out, w_ada, b_ada, ln1_g, ln1_b, ln2_g, ln2_b, w_ff1, b_ff1, w_ff2, b_ff2):
    n_layers, d, _ = w_in.shape
    bp, tp, _ = x_prompt.shape
    bs, tsamp, _ = x_sample.shape
    past = cache_k.shape[2]
    w = N_HEADS * HEAD_W
    alpha = (2.0 * n_layers) ** 0.25

    w_in_b = w_in.astype(BF16)
    w_ao_b = w_attn_out.astype(BF16)
    w_co_b = w_conv_out.astype(BF16)
    w_o_b = w_out.astype(BF16)
    w_ada_b = w_ada.astype(BF16)
    w1_b = w_ff1.astype(BF16)
    w2_b = w_ff2.astype(BF16)

    c_all = jnp.concatenate([c_prompt, c_sample], axis=0)
    ada_all = _ada_call(c_all, w_ada_b, b_ada[:, None, :]).reshape(n_layers, bp + bs, 6, d)

    cache_kt = jnp.transpose(cache_k, (0, 1, 3, 4, 5, 2)).reshape(n_layers, bs, w, past)
    pc = _div_tile(past, 1024)
    zero_past = jnp.zeros((bp, CONV_W - 1, d), F32)

    bt_p, ts_p = _row_tiles(bp, tp, 1024)
    bt_s, ts_s = _row_tiles(bs, tsamp, 1024)
    mbt_p, mts_p = _row_tiles(bp, tp, 512)
    mbt_s, mts_s = _row_tiles(bs, tsamp, 512)
    tn = _div_tile(w, 256)
    tf = _div_tile(w_ff1.shape[2], 512)
    tq = _div_tile(tp, 512)

    xp, xs = x_prompt, x_sample
    outs = [[] for _ in range(6)]
    for layer in range(n_layers):
        lam_init = 0.8 - 0.6 * math.exp(-0.3 * layer)
        ada_p = ada_all[layer, :bp]
        ada_s = ada_all[layer, bp:]
        lams = (lam_q1[layer][None], lam_k1[layer][None], lam_q2[layer][None], lam_k2[layer][None],
                subln_g[layer][None])
        ln1 = (ln1_g[layer][None], ln1_b[layer][None])
        ffn_w = (w1_b[layer], b_ff1[layer][None], w2_b[layer], b_ff2[layer][None],
                 ln2_g[layer][None], ln2_b[layer][None])

        q, kf, kb, vf, vb, yb, sga, sgb, nc = _proj_call(
            xp, ada_p, w_in_b[layer], conv_w[layer], zero_past,
            bt=bt_p, ts=ts_p, tn=tn, feature_major=True, name=f"proj_p{layer}")
        on = _pattn_call(q, kb, vb, *lams, tq=tq, lam_init=lam_init, name=f"attn_p{layer}")
        xp = _mix_call(xp, ada_p, on, yb, sga, sgb, w_ao_b[layer], w_co_b[layer], w_o_b[layer], *ln1,
                       bt=mbt_p, ts=mts_p, alpha=alpha, name=f"mix_p{layer}")
        xp = _ffn_call(xp, ada_p, *ffn_w, bt=bt_p, ts=ts_p, tf=tf, alpha=alpha, name=f"ffn_p{layer}")
        outs[0].append(kf)
        outs[1].append(vf)
        outs[2].append(nc[:, -1])

        q, kf, kb, vf, vb, yb, sga, sgb, nc = _proj_call(
            xs, ada_s, w_in_b[layer], conv_w[layer], state_conv[layer],
            bt=bt_s, ts=ts_s, tn=tn, feature_major=False, name=f"proj_s{layer}")
        on = _sattn_call(q, cache_kt, cache_v, kb, vb, *lams, layer=layer, pc=pc, lam_init=lam_init,
                         name=f"attn_s{layer}")
        xs = _mix_call(xs, ada_s, on, yb, sga, sgb, w_ao_b[layer], w_co_b[layer], w_o_b[layer], *ln1,
                       bt=mbt_s, ts=mts_s, alpha=alpha, name=f"mix_s{layer}")
        xs = _ffn_call(xs, ada_s, *ffn_w, bt=bt_s, ts=ts_s, tf=tf, alpha=alpha, name=f"ffn_s{layer}")
        outs[3].append(kf)
        outs[4].append(vf)
        outs[5].append(nc[:, -1])

    new_k_p = jnp.stack(outs[0]).reshape(n_layers, bp, tp, N_HEADS, 2, HEAD_DIM)
    new_v_p = jnp.stack(outs[1]).reshape(n_layers, bp, tp, N_HEADS, V_DIM)
    new_c_p = jnp.stack(outs[2])
    new_k_s = jnp.stack(outs[3]).reshape(n_layers, bs, tsamp, N_HEADS, 2, HEAD_DIM)
    new_v_s = jnp.stack(outs[4]).reshape(n_layers, bs, tsamp, N_HEADS, V_DIM)
    new_c_s = jnp.stack(outs[5])
    return (xp, xs, new_k_p, new_v_p, new_c_p, new_k_s, new_v_s, new_c_s)
```

```python
import functools
import math

import jax
import jax.numpy as jnp
from jax import lax
from jax.experimental import pallas as pl
from jax.experimental.pallas import tpu as pltpu

N_HEADS = 8
HEAD_DIM = 64
V_DIM = 2 * HEAD_DIM
HEAD_W = 2 * HEAD_DIM
CHUNK = 64
CONV_W = 3
LN_EPS = 1e-5
NEG_INF = -1e30
N_SEG = 8
Q_SCALE = (HEAD_DIM ** -0.5) * math.log2(math.e)
ONES_ROWS = 16

F32 = jnp.float32
BF16 = jnp.bfloat16

VMEM_LIMIT = 56 * 1024 * 1024


def _cparams(n_axes):
    return pltpu.CompilerParams(dimension_semantics=("arbitrary",) * n_axes,
                                vmem_limit_bytes=VMEM_LIMIT)


def _ln(x):
    mu = jnp.mean(x, axis=-1, keepdims=True)
    xc = x - mu
    var = jnp.mean(xc * xc, axis=-1, keepdims=True)
    return xc * lax.rsqrt(var + LN_EPS)


def _dot(a, b):
    return jnp.dot(a, b, preferred_element_type=F32)


def _dot_nt(a, b):
    return lax.dot_general(a, b, (((1,), (1,)), ((), ())), preferred_element_type=F32)


def _ada_kernel(c_ref, w_ref, b_ref, o_ref):
    c = c_ref[...]
    sc = c * jax.nn.sigmoid(c)
    o_ref[...] = _dot(sc.astype(BF16), w_ref[...]) + b_ref[...]


def _ada_call(c_all, w_ada, b_ada):
    n_layers, d, d6 = w_ada.shape
    nb = c_all.shape[0]
    tn = d
    return pl.pallas_call(
        _ada_kernel,
        out_shape=jax.ShapeDtypeStruct((n_layers, nb, d6), F32),
        grid=(n_layers, d6 // tn),
        in_specs=[
            pl.BlockSpec((nb, d), lambda l, j: (0, 0)),
            pl.BlockSpec((None, d, tn), lambda l, j: (l, 0, j)),
            pl.BlockSpec((None, 1, tn), lambda l, j: (l, 0, j)),
        ],
        out_specs=pl.BlockSpec((None, nb, tn), lambda l, j: (l, 0, j)),
        compiler_params=_cparams(2),
        name="ada",
    )(c_all, w_ada, b_ada)


def _proj_kernel(x_ref, ada_ref, wq, wk, wv, wgb, wgc, wh, wga, wgg, cw_ref, past_ref,
                 q_ref, kf_ref, kb_ref, vf_ref, vb_ref, yb_ref, sga_ref, sgb_ref, nc_ref,
                 u_scr, carry_scr, *, bt, ts, feature_major):
    t = pl.program_id(1)
    j = pl.program_id(2)
    d = x_ref.shape[-1]
    tn = wq.shape[-1]
    m = bt * ts

    def attn_layout(a):
        if feature_major:
            return a.T.astype(BF16).reshape(bt, tn, ts)
        return a.astype(BF16).reshape(bt, ts, tn)

    @pl.when(j == 0)
    def _():
        xn = _ln(x_ref[...])
        u = xn * (1.0 + ada_ref[:, 1:2, :]) + ada_ref[:, 0:1, :]
        u_scr[...] = u.reshape(m, d).astype(BF16)

    @pl.when(t == 0)
    def _():
        carry_scr[j] = past_ref[...]

    u = u_scr[...]

    def seg(w_ref):
        return _dot(u, w_ref[...])

    q_ref[...] = attn_layout(seg(wq) * Q_SCALE)
    k = seg(wk).reshape(bt, ts, tn)
    kf_ref[...] = k
    kb_ref[...] = k.astype(BF16)
    v = seg(wv)
    vf_ref[...] = v.reshape(bt, ts, tn)
    vb_ref[...] = attn_layout(v)

    z2 = seg(wgc) * seg(wh)
    z = z2.reshape(bt, ts, tn)
    prev = carry_scr[j]
    p0 = prev[:, 0:1, :]
    p1 = prev[:, 1:2, :]
    row = lax.broadcasted_iota(jnp.int32, (bt, ts, tn), 1)
    zr1 = pltpu.roll(z2, 1, 0).reshape(bt, ts, tn)
    zr2 = pltpu.roll(z2, 2, 0).reshape(bt, ts, tn)
    s1 = jnp.where(row == 0, p1, zr1)
    s2 = jnp.where(row == 0, p0, jnp.where(row == 1, p1, zr2))
    cw = cw_ref[...]
    conv = cw[0:1, :].reshape(1, 1, tn) * s2 + cw[1:2, :].reshape(1, 1, tn) * s1
    conv = conv + cw[2:3, :].reshape(1, 1, tn) * z
    tail = z[:, ts - (CONV_W - 1):, :]
    carry_scr[j] = tail
    nc_ref[...] = tail

    yb_ref[...] = (seg(wgb).reshape(bt, ts, tn) * conv).astype(BF16)
    sga_ref[...] = jax.nn.sigmoid(seg(wga)).astype(BF16).reshape(bt, ts, tn)
    sgb_ref[...] = jax.nn.sigmoid(seg(wgg)).astype(BF16).reshape(bt, ts, tn)


def _proj_call(x, ada, w_in, conv_w, conv_past, *, bt, ts, tn, feature_major, name):
    b, t, d = x.shape
    w = w_in.shape[1] // N_SEG
    nj = w // tn
    grid = (b // bt, t // ts, nj)

    def wspec(s):
        return pl.BlockSpec((d, tn), lambda bi, ti, j, s=s: (0, s * nj + j))

    act = pl.BlockSpec((bt, ts, tn), lambda bi, ti, j: (bi, ti, j))
    sds = jax.ShapeDtypeStruct
    if feature_major:
        attn_act = pl.BlockSpec((bt, tn, ts), lambda bi, ti, j: (bi, j, ti))
        attn_shape = sds((b, w, t), BF16)
    else:
        attn_act, attn_shape = act, sds((b, t, w), BF16)
    tail = pl.BlockSpec((bt, CONV_W - 1, tn), lambda bi, ti, j: (bi, 0, j))
    tails = pl.BlockSpec((bt, None, CONV_W - 1, tn), lambda bi, ti, j: (bi, ti, 0, j))
    out_shape = (attn_shape, sds((b, t, w), F32), sds((b, t, w), BF16),
                 sds((b, t, w), F32), attn_shape, sds((b, t, w), BF16),
                 sds((b, t, w), BF16), sds((b, t, w), BF16), sds((b, t // ts, CONV_W - 1, w), F32))
    return pl.pallas_call(
        functools.partial(_proj_kernel, bt=bt, ts=ts, feature_major=feature_major),
        out_shape=out_shape,
        grid=grid,
        in_specs=[
            pl.BlockSpec((bt, ts, d), lambda bi, ti, j: (bi, ti, 0)),
            pl.BlockSpec((bt, 6, d), lambda bi, ti, j: (bi, 0, 0)),
            *[wspec(s) for s in range(N_SEG)],
            pl.BlockSpec((CONV_W, tn), lambda bi, ti, j: (0, j)),
            tail,
        ],
        out_specs=(attn_act, act, act, act, attn_act, act, act, act, tails),
        scratch_shapes=[pltpu.VMEM((bt * ts, d), BF16),
                        pltpu.VMEM((nj, bt, CONV_W - 1, tn), F32)],
        compiler_params=_cparams(3),
        name=name,
    )(x, ada, *([w_in] * N_SEG), conv_w, conv_past)


def _stack_q(q):
    lane = lax.broadcasted_iota(jnp.int32, q.shape, 1)
    zero = jnp.zeros_like(q)
    return jnp.concatenate([jnp.where(lane < HEAD_DIM, q, zero),
                            jnp.where(lane >= HEAD_DIM, q, zero)], axis=0)


def _lam(lq1, lk1, lq2, lk2, lam_init):
    s1 = jnp.sum(lq1[...] * lk1[...], axis=-1, keepdims=True)
    s2 = jnp.sum(lq2[...] * lk2[...], axis=-1, keepdims=True)
    return jnp.exp(s1) - jnp.exp(s2) + lam_init


def _finish_heads(o1, o2, lam, g, lam_init):
    o = o1 - lam * o2
    o = o * lax.rsqrt(jnp.mean(o * o, axis=-1, keepdims=True) + LN_EPS)
    return o * g * (1.0 - lam_init)


def _pattn_kernel(qt_ref, k_ref, vt_ref, vis_a, vis_b, lq1, lk1, lq2, lk2, g_ref, o_ref,
                  qs_scr, acc_scr, sa_scr, sb_scr, *, tq, tk, lam_init):
    qi = pl.program_id(2)
    half = tq // 2
    qt = qt_ref[...]
    feat = lax.broadcasted_iota(jnp.int32, qt.shape, 0)
    zero = jnp.zeros_like(qt)
    qs_scr[:, :tq] = jnp.where(feat < HEAD_DIM, qt, zero)
    qs_scr[:, tq:] = jnp.where(feat >= HEAD_DIM, qt, zero)
    acc_scr[...] = jnp.zeros(acc_scr.shape, F32)
    ones = jnp.ones((ONES_ROWS, tk), BF16)

    def key_start(j):
        return pl.multiple_of(j * tk, tk)

    def scores(j, dst):
        dst[...] = _dot(k_ref[pl.ds(key_start(j), tk), :], qs_scr[...])

    def softmax_pv(j, s, m_old):
        m_new = jnp.maximum(m_old, jnp.max(s, axis=0, keepdims=True))
        a = jnp.exp2(m_old - m_new)
        p = jnp.exp2(s - m_new).astype(BF16)
        vt1 = jnp.concatenate([vt_ref[:, pl.ds(key_start(j), tk)], ones], axis=0)
        return m_new, a, _dot(vt1, p)

    def update(j, s, m_old):
        m_new, a, pv = softmax_pv(j, s, m_old)
        acc_scr[...] = a * acc_scr[...] + pv
        return m_new

    m = jnp.full((1, 2 * tq), NEG_INF, F32)
    npair = (qi * tq) // (2 * tk)
    scores(0, sa_scr)

    def body(i, m):
        j = 2 * i
        scores(j + 1, sb_scr)
        m = update(j, sa_scr[...], m)
        scores(j + 2, sa_scr)
        return update(j + 1, sb_scr[...], m)

    m = lax.fori_loop(0, npair, body, m)
    j = 2 * npair
    late = lambda a: jnp.concatenate([a[:, half:tq], a[:, tq + half:]], axis=1)
    sb_scr[:, :tq] = _dot(k_ref[pl.ds(key_start(j + 1), tk), :], late(qs_scr[...]))
    m = update(j, jnp.where(vis_a[...] > 0.5, sa_scr[...], NEG_INF), m)
    _, a, pv = softmax_pv(j + 1, jnp.where(vis_b[...] > 0.5, sb_scr[:, :tq], NEG_INF), late(m))
    acc_scr[:, half:tq] = a[:, :half] * acc_scr[:, half:tq] + pv[:, :half]
    acc_scr[:, tq + half:] = a[:, half:] * acc_scr[:, tq + half:] + pv[:, half:]

    o = acc_scr[:V_DIM, :] / acc_scr[V_DIM:V_DIM + 1, :]
    lam = _lam(lq1, lk1, lq2, lk2, lam_init)
    o = o[:, :tq] - lam * o[:, tq:]
    o = o * lax.rsqrt(jnp.mean(o * o, axis=0, keepdims=True) + LN_EPS)
    o_ref[...] = (o.T * g_ref[...] * (1.0 - lam_init)).astype(o_ref.dtype)


def _pattn_call(qt, k, vt, lq1, lk1, lq2, lk2, g, *, tq, lam_init, name):
    b, t, w = k.shape
    nh = w // HEAD_W
    tk = tq // 2
    assert tk % CHUNK == 0
    kchunk = lax.broadcasted_iota(jnp.int32, (tk, 2 * tq), 0) // CHUNK
    col = lax.broadcasted_iota(jnp.int32, (tk, 2 * tq), 1)
    vis_a = (kchunk <= (col % tq) // CHUNK).astype(F32)
    vis_b = (kchunk[:, :tq] <= (col[:, :tq] % tk) // CHUNK).astype(F32)
    vec = lambda n: pl.BlockSpec((1, n), lambda bi, h, qi: (0, 0))
    return pl.pallas_call(
        functools.partial(_pattn_kernel, tq=tq, tk=tk, lam_init=lam_init),
        out_shape=jax.ShapeDtypeStruct((b, t, w), BF16),
        grid=(b, nh, t // tq),
        in_specs=[
            pl.BlockSpec((None, HEAD_W, tq), lambda bi, h, qi: (bi, h, qi)),
            pl.BlockSpec((None, t, HEAD_W), lambda bi, h, qi: (bi, 0, h)),
            pl.BlockSpec((None, HEAD_W, t), lambda bi, h, qi: (bi, h, 0)),
            pl.BlockSpec((tk, 2 * tq), lambda bi, h, qi: (0, 0)),
            pl.BlockSpec((tk, tq), lambda bi, h, qi: (0, 0)),
            vec(HEAD_DIM), vec(HEAD_DIM), vec(HEAD_DIM), vec(HEAD_DIM), vec(V_DIM),
        ],
        out_specs=pl.BlockSpec((None, tq, HEAD_W), lambda bi, h, qi: (bi, qi, h)),
        scratch_shapes=[pltpu.VMEM((HEAD_W, 2 * tq), BF16),
                        pltpu.VMEM((V_DIM + ONES_ROWS, 2 * tq), F32),
                        pltpu.VMEM((tk, 2 * tq), F32),
                        pltpu.VMEM((tk, 2 * tq), F32)],
        compiler_params=_cparams(3),
        name=name,
    )(qt, k, vt, vis_a, vis_b, lq1, lk1, lq2, lk2, g)


def _sattn_kernel(q_ref, kt_ref, vc_ref, kn_ref, vn_ref, lq1, lk1, lq2, lk2, g_ref, o_ref,
                  qs_scr, m_scr, l_scr, acc_scr, *, lam_init):
    c = pl.program_id(1)
    ts = q_ref.shape[0]
    nh = acc_scr.shape[0]

    def hs(h):
        return slice(h * HEAD_W, (h + 1) * HEAD_W)

    @pl.when(c == 0)
    def _():
        for h in range(nh):
            qs = _stack_q(q_ref[:, hs(h)])
            qs_scr[h] = qs
            s = _dot_nt(qs, kn_ref[:, hs(h)])
            m = jnp.max(s, axis=-1, keepdims=True)
            p = jnp.exp2(s - m)
            m_scr[h] = m
            l_scr[h] = jnp.sum(p, axis=-1, keepdims=True)
            acc_scr[h] = _dot(p.astype(BF16), vn_ref[:, hs(h)])

    vh = pltpu.einshape("phd->hpd", vc_ref[...]).astype(BF16)
    for h in range(nh):
        s = _dot(qs_scr[h], kt_ref[hs(h), :].astype(BF16))
        m_old = m_scr[h]
        m_new = jnp.maximum(m_old, jnp.max(s, axis=-1, keepdims=True))
        a = jnp.exp2(m_old - m_new)
        p = jnp.exp2(s - m_new)
        m_scr[h] = m_new
        l_scr[h] = a * l_scr[h] + jnp.sum(p, axis=-1, keepdims=True)
        acc_scr[h] = a * acc_scr[h] + _dot(p.astype(BF16), vh[h])

    @pl.when(c == pl.num_programs(1) - 1)
    def _():
        lam = _lam(lq1, lk1, lq2, lk2, lam_init)
        for h in range(nh):
            o = acc_scr[h] / l_scr[h]
            o_ref[:, hs(h)] = _finish_heads(o[:ts], o[ts:], lam, g_ref[...], lam_init).astype(o_ref.dtype)


def _sattn_call(q, cache_kt, cache_v, k_new, v_new, lq1, lk1, lq2, lk2, g, *, layer, pc, lam_init, name):
    b, t, w = q.shape
    p = cache_kt.shape[3]
    nh = w // HEAD_W
    new = pl.BlockSpec((None, t, w), lambda bi, c: (bi, 0, 0))
    vec = lambda n: pl.BlockSpec((1, n), lambda bi, c: (0, 0))
    return pl.pallas_call(
        functools.partial(_sattn_kernel, lam_init=lam_init),
        out_shape=jax.ShapeDtypeStruct((b, t, w), BF16),
        grid=(b, p // pc),
        in_specs=[new,
                  pl.BlockSpec((None, None, w, pc), lambda bi, c: (layer, bi, 0, c)),
                  pl.BlockSpec((None, None, pc, nh, V_DIM), lambda bi, c: (layer, bi, c, 0, 0)),
                  new, new,
                  vec(HEAD_DIM), vec(HEAD_DIM), vec(HEAD_DIM), vec(HEAD_DIM), vec(V_DIM)],
        out_specs=new,
        scratch_shapes=[pltpu.VMEM((nh, 2 * t, HEAD_W), BF16),
                        pltpu.VMEM((nh, 2 * t, 1), F32),
                        pltpu.VMEM((nh, 2 * t, 1), F32),
                        pltpu.VMEM((nh, 2 * t, V_DIM), F32)],
        compiler_params=_cparams(2),
        name=name,
    )(q, cache_kt, cache_v, k_new, v_new, lq1, lk1, lq2, lk2, g)


def _mix_kernel(x_ref, ada_ref, on_ref, yb_ref, sga_ref, sgb_ref, wa_ref, wc_ref, wo_ref,
                g_ref, b_ref, out_ref, *, bt, ts, alpha):
    d = x_ref.shape[-1]
    m = bt * ts
    ya = _dot(on_ref[...].reshape(m, d), wa_ref[...])
    yb = _dot(yb_ref[...].reshape(m, d), wc_ref[...])
    merged = (sga_ref[...].reshape(m, d).astype(F32) * ya
              + sgb_ref[...].reshape(m, d).astype(F32) * yb)
    mix = _dot(merged.astype(BF16), wo_ref[...]).reshape(bt, ts, d)
    y = _ln(alpha * x_ref[...] + ada_ref[:, 2:3, :] * mix)
    out_ref[...] = y * g_ref[...].reshape(1, 1, d) + b_ref[...].reshape(1, 1, d)


def _mix_call(x, ada, on, yb, sga, sgb, wa, wc, wo, g, bln, *, bt, ts, alpha, name):
    b, t, d = x.shape
    act = pl.BlockSpec((bt, ts, d), lambda bi, ti: (bi, ti, 0))
    wsp = pl.BlockSpec((d, d), lambda bi, ti: (0, 0))
    vec = pl.BlockSpec((1, d), lambda bi, ti: (0, 0))
    return pl.pallas_call(
        functools.partial(_mix_kernel, bt=bt, ts=ts, alpha=alpha),
        out_shape=jax.ShapeDtypeStruct((b, t, d), F32),
        grid=(b // bt, t // ts),
        in_specs=[act, pl.BlockSpec((bt, 6, d), lambda bi, ti: (bi, 0, 0)),
                  act, act, act, act, wsp, wsp, wsp, vec, vec],
        out_specs=act,
        compiler_params=_cparams(2),
        name=name,
    )(x, ada, on, yb, sga, sgb, wa, wc, wo, g, bln)


def _ffn_kernel(x_ref, ada_ref, w1_ref, b1_ref, w2_ref, b2_ref, g_ref, b_ref, out_ref,
                u_scr, acc_scr, *, bt, ts, alpha):
    f = pl.program_id(2)
    d = x_ref.shape[-1]
    m = bt * ts

    @pl.when(f == 0)
    def _():
        u = _ln(x_ref[...]) * (1.0 + ada_ref[:, 4:5, :]) + ada_ref[:, 3:4, :]
        u_scr[...] = u.reshape(m, d).astype(BF16)
        acc_scr[...] = jnp.zeros(acc_scr.shape, F32)

    hid = jnp.maximum(_dot(u_scr[...], w1_ref[...]) + b1_ref[...], 0.0)
    acc_scr[...] += _dot((hid * hid).astype(BF16), w2_ref[...])

    @pl.when(f == pl.num_programs(2) - 1)
    def _():
        ff = (acc_scr[...] + b2_ref[...]).reshape(bt, ts, d)
        y = _ln(alpha * x_ref[...] + ada_ref[:, 5:6, :] * ff)
        out_ref[...] = y * g_ref[...].reshape(1, 1, d) + b_ref[...].reshape(1, 1, d)


def _ffn_call(x, ada, w1, b1, w2, b2, g, bln, *, bt, ts, tf, alpha, name):
    b, t, d = x.shape
    dff = w1.shape[1]
    act = pl.BlockSpec((bt, ts, d), lambda bi, ti, f: (bi, ti, 0))
    vec = pl.BlockSpec((1, d), lambda bi, ti, f: (0, 0))
    return pl.pallas_call(
        functools.partial(_ffn_kernel, bt=bt, ts=ts, alpha=alpha),
        out_shape=jax.ShapeDtypeStruct((b, t, d), F32),
        grid=(b // bt, t // ts, dff // tf),
        in_specs=[act, pl.BlockSpec((bt, 6, d), lambda bi, ti, f: (bi, 0, 0)),
                  pl.BlockSpec((d, tf), lambda bi, ti, f: (0, f)),
                  pl.BlockSpec((1, tf), lambda bi, ti, f: (0, f)),
                  pl.BlockSpec((tf, d), lambda bi, ti, f: (f, 0)),
                  vec, vec, vec],
        out_specs=act,
        scratch_shapes=[pltpu.VMEM((bt * ts, d), BF16), pltpu.VMEM((bt * ts, d), F32)],
        compiler_params=_cparams(3),
        name=name,
    )(x, ada, w1, b1, w2, b2, g, bln)


def _row_tiles(b, t, rows):
    if t >= rows:
        ts = rows
        while t % ts:
            ts //= 2
        return 1, ts
    bt = max(1, min(b, rows // t))
    while b % bt:
        bt -= 1
    return bt, t


def _div_tile(n, want):
    tile = min(n, want)
    while n % tile:
        tile //= 2
    return tile


def kernel(x_prompt, x_sample, c_prompt, c_sample, cache_k, cache_v, state_conv, w_in, lam_q1, lam_k1, lam_q2, lam_k2, subln_g, w_attn_out, conv_w, w_conv_out, w_out, w_ada, b_ada, ln1_g, ln1_b, ln2_g, ln2_b, w_ff1, b_ff1, w_ff2, b_ff2):
    n_layers, d, _ = w_in.shape
    bp, tp, _ = x_prompt.shape
    bs, tsamp, _ = x_sample.shape
    past = cache_k.shape[2]
    w = N_HEADS * HEAD_W
    alpha = (2.0 * n_layers) ** 0.25

    w_in_b = w_in.astype(BF16)
    w_ao_b = w_attn_out.astype(BF16)
    w_co_b = w_conv_out.astype(BF16)
    w_o_b = w_out.astype(BF16)
    w_ada_b = w_ada.astype(BF16)
    w1_b = w_ff1.astype(BF16)
    w2_b = w_ff2.astype(BF16)

    c_all = jnp.concatenate([c_prompt, c_sample], axis=0)
    ada_all = _ada_call(c_all, w_ada_b, b_ada[:, None, :]).reshape(n_layers, bp + bs, 6, d)

    cache_kt = jnp.transpose(cache_k, (0, 1, 3, 4, 5, 2)).reshape(n_layers, bs, w, past)
    pc = _div_tile(past, 1024)
    zero_past = jnp.zeros((bp, CONV_W - 1, d), F32)

    bt_p, ts_p = _row_tiles(bp, tp, 1024)
    bt_s, ts_s = _row_tiles(bs, tsamp, 1024)
    mbt_p, mts_p = _row_tiles(bp, tp, 512)
    mbt_s, mts_s = _row_tiles(bs, tsamp, 512)
    tn = _div_tile(w, 256)
    tf = _div_tile(w_ff1.shape[2], 512)
    tq = _div_tile(tp, 1024)

    xp, xs = x_prompt, x_sample
    outs = [[] for _ in range(6)]
    for layer in range(n_layers):
        lam_init = 0.8 - 0.6 * math.exp(-0.3 * layer)
        ada_p = ada_all[layer, :bp]
        ada_s = ada_all[layer, bp:]
        lams = (lam_q1[layer][None], lam_k1[layer][None], lam_q2[layer][None], lam_k2[layer][None],
                subln_g[layer][None])
        ln1 = (ln1_g[layer][None], ln1_b[layer][None])
        ffn_w = (w1_b[layer], b_ff1[layer][None], w2_b[layer], b_ff2[layer][None],
                 ln2_g[layer][None], ln2_b[layer][None])

        q, kf, kb, vf, vb, yb, sga, sgb, nc = _proj_call(
            xp, ada_p, w_in_b[layer], conv_w[layer], zero_past,
            bt=bt_p, ts=ts_p, tn=tn, feature_major=True, name=f"proj_p{layer}")
        on = _pattn_call(q, kb, vb, *lams, tq=tq, lam_init=lam_init, name=f"attn_p{layer}")
        xp = _mix_call(xp, ada_p, on, yb, sga, sgb, w_ao_b[layer], w_co_b[layer], w_o_b[layer], *ln1,
                       bt=mbt_p, ts=mts_p, alpha=alpha, name=f"mix_p{layer}")
        xp = _ffn_call(xp, ada_p, *ffn_w, bt=bt_p, ts=ts_p, tf=tf, alpha=alpha, name=f"ffn_p{layer}")
        outs[0].append(kf)
        outs[1].append(vf)
        outs[2].append(nc[:, -1])

        q, kf, kb, vf, vb, yb, sga, sgb, nc = _proj_call(
            xs, ada_s, w_in_b[layer], conv_w[layer], state_conv[layer],
            bt=bt_s, ts=ts_s, tn=tn, feature_major=False, name=f"proj_s{layer}")
        on = _sattn_call(q, cache_kt, cache_v, kb, vb, *lams, layer=layer, pc=pc, lam_init=lam_init,
                         name=f"attn_s{layer}")
        xs = _mix_call(xs, ada_s, on, yb, sga, sgb, w_ao_b[layer], w_co_b[layer], w_o_b[layer], *ln1,
                       bt=mbt_s, ts=mts_s, alpha=alpha, name=f"mix_s{layer}")
        xs = _ffn_call(xs, ada_s, *ffn_w, bt=bt_s, ts=ts_s, tf=tf, alpha=alpha, name=f"ffn_s{layer}")
        outs[3].append(kf)
        outs[4].append(vf)
        outs[5].append(nc[:, -1])

    new_k_p = jnp.stack(outs[0]).reshape(n_layers, bp, tp, N_HEADS, 2, HEAD_DIM)
    new_v_p = jnp.stack(outs[1]).reshape(n_layers, bp, tp, N_HEADS, V_DIM)
    new_c_p = jnp.stack(outs[2])
    new_k_s = jnp.stack(outs[3]).reshape(n_layers, bs, tsamp, N_HEADS, 2, HEAD_DIM)
    new_v_s = jnp.stack(outs[4]).reshape(n_layers, bs, tsamp, N_HEADS, V_DIM)
    new_c_s = jnp.stack(outs[5])
    return (xp, xs, new_k_p, new_v_p, new_c_p, new_k_s, new_v_s, new_c_s)
```

```python
import functools
import math

import jax
import jax.numpy as jnp
from jax import lax
from jax.experimental import pallas as pl
from jax.experimental.pallas import tpu as pltpu

N_HEADS = 8
HEAD_DIM = 64
V_DIM = 2 * HEAD_DIM
HEAD_W = 2 * HEAD_DIM
CHUNK = 64
CONV_W = 3
LN_EPS = 1e-5
NEG_INF = -1e30
N_SEG = 8
Q_SCALE = (HEAD_DIM ** -0.5) * math.log2(math.e)
ONES_ROWS = 16

F32 = jnp.float32
BF16 = jnp.bfloat16

VMEM_LIMIT = 56 * 1024 * 1024


def _cparams(n_axes):
    return pltpu.CompilerParams(dimension_semantics=("arbitrary",) * n_axes,
                                vmem_limit_bytes=VMEM_LIMIT)


def _ln(x):
    mu = jnp.mean(x, axis=-1, keepdims=True)
    xc = x - mu
    var = jnp.mean(xc * xc, axis=-1, keepdims=True)
    return xc * lax.rsqrt(var + LN_EPS)


def _dot(a, b):
    return jnp.dot(a, b, preferred_element_type=F32)


def _dot_nt(a, b):
    return lax.dot_general(a, b, (((1,), (1,)), ((), ())), preferred_element_type=F32)


def _ada_kernel(c_ref, w_ref, b_ref, o_ref):
    c = c_ref[...]
    sc = c * jax.nn.sigmoid(c)
    o_ref[...] = _dot(sc.astype(BF16), w_ref[...]) + b_ref[...]


def _ada_call(c_all, w_ada, b_ada):
    n_layers, d, d6 = w_ada.shape
    nb = c_all.shape[0]
    tn = d
    return pl.pallas_call(
        _ada_kernel,
        out_shape=jax.ShapeDtypeStruct((n_layers, nb, d6), F32),
        grid=(n_layers, d6 // tn),
        in_specs=[
            pl.BlockSpec((nb, d), lambda l, j: (0, 0)),
            pl.BlockSpec((None, d, tn), lambda l, j: (l, 0, j)),
            pl.BlockSpec((None, 1, tn), lambda l, j: (l, 0, j)),
        ],
        out_specs=pl.BlockSpec((None, nb, tn), lambda l, j: (l, 0, j)),
        compiler_params=_cparams(2),
        name="ada",
    )(c_all, w_ada, b_ada)


def _proj_kernel(x_ref, ada_ref, wq, wk, wv, wgb, wgc, wh, wga, wgg, cw_ref, past_ref, *rest,
                 bt, ts, feature_major):
    if feature_major:
        rest = rest[1:]
    (q_ref, kf_ref, kb_ref, vf_ref, vb_ref, yb_ref, sga_ref, sgb_ref, nc_ref, u_scr, carry_scr) = rest
    t = pl.program_id(1)
    j = pl.program_id(2)
    d = x_ref.shape[-1]
    tn = wq.shape[-1]
    m = bt * ts

    def attn_layout(a):
        if feature_major:
            return a.T.astype(BF16).reshape(bt, tn, ts)
        return a.astype(BF16).reshape(bt, ts, tn)

    @pl.when(j == 0)
    def _():
        xn = _ln(x_ref[...])
        u = xn * (1.0 + ada_ref[:, 1:2, :]) + ada_ref[:, 0:1, :]
        u_scr[...] = u.reshape(m, d).astype(BF16)

    @pl.when(t == 0)
    def _():
        carry_scr[j] = past_ref[...]

    u = u_scr[...]

    def seg(w_ref):
        return _dot(u, w_ref[...])

    q_ref[...] = attn_layout(seg(wq) * Q_SCALE)
    k = seg(wk)
    kf_ref[...] = k.T.reshape(bt, tn, ts) if feature_major else k.reshape(bt, ts, tn)
    kb_ref[...] = k.astype(BF16).reshape(bt, ts, tn)
    v = seg(wv)
    vf_ref[...] = v.reshape(bt, ts, tn)
    vb_ref[...] = attn_layout(v)

    z2 = seg(wgc) * seg(wh)
    z = z2.reshape(bt, ts, tn)
    prev = carry_scr[j]
    p0 = prev[:, 0:1, :]
    p1 = prev[:, 1:2, :]
    row = lax.broadcasted_iota(jnp.int32, (bt, ts, tn), 1)
    zr1 = pltpu.roll(z2, 1, 0).reshape(bt, ts, tn)
    zr2 = pltpu.roll(z2, 2, 0).reshape(bt, ts, tn)
    s1 = jnp.where(row == 0, p1, zr1)
    s2 = jnp.where(row == 0, p0, jnp.where(row == 1, p1, zr2))
    cw = cw_ref[...]
    conv = cw[0:1, :].reshape(1, 1, tn) * s2 + cw[1:2, :].reshape(1, 1, tn) * s1
    conv = conv + cw[2:3, :].reshape(1, 1, tn) * z
    tail = z[:, ts - (CONV_W - 1):, :]
    carry_scr[j] = tail
    nc_ref[...] = tail

    yb_ref[...] = (seg(wgb).reshape(bt, ts, tn) * conv).astype(BF16)
    sga_ref[...] = jax.nn.sigmoid(seg(wga)).astype(BF16).reshape(bt, ts, tn)
    sgb_ref[...] = jax.nn.sigmoid(seg(wgg)).astype(BF16).reshape(bt, ts, tn)


def _proj_call(x, ada, w_in, conv_w, conv_past, *, bt, ts, tn, name, k_stack=None, layer=0):
    b, t, d = x.shape
    w = w_in.shape[1] // N_SEG
    nj = w // tn
    grid = (b // bt, t // ts, nj)
    feature_major = k_stack is not None

    def wspec(s):
        return pl.BlockSpec((d, tn), lambda bi, ti, j, s=s: (0, s * nj + j))

    act = pl.BlockSpec((bt, ts, tn), lambda bi, ti, j: (bi, ti, j))
    sds = jax.ShapeDtypeStruct
    extra_in, extra_specs, aliases = (), (), {}
    if feature_major:
        attn_act = pl.BlockSpec((bt, tn, ts), lambda bi, ti, j: (bi, j, ti))
        attn_shape = sds((b, w, t), BF16)
        kf_act = pl.BlockSpec((None, bt, tn, ts), lambda bi, ti, j: (layer, bi, j, ti))
        kf_shape = sds(k_stack.shape, F32)
        extra_in, extra_specs = (k_stack,), (pl.BlockSpec(memory_space=pl.ANY),)
        aliases = {4 + N_SEG: 1}
    else:
        attn_act, attn_shape = act, sds((b, t, w), BF16)
        kf_act, kf_shape = act, sds((b, t, w), F32)
    tail = pl.BlockSpec((bt, CONV_W - 1, tn), lambda bi, ti, j: (bi, 0, j))
    tails = pl.BlockSpec((bt, None, CONV_W - 1, tn), lambda bi, ti, j: (bi, ti, 0, j))
    out_shape = (attn_shape, kf_shape, sds((b, t, w), BF16),
                 sds((b, t, w), F32), attn_shape, sds((b, t, w), BF16),
                 sds((b, t, w), BF16), sds((b, t, w), BF16), sds((b, t // ts, CONV_W - 1, w), F32))
    return pl.pallas_call(
        functools.partial(_proj_kernel, bt=bt, ts=ts, feature_major=feature_major),
        out_shape=out_shape,
        grid=grid,
        in_specs=[
            pl.BlockSpec((bt, ts, d), lambda bi, ti, j: (bi, ti, 0)),
            pl.BlockSpec((bt, 6, d), lambda bi, ti, j: (bi, 0, 0)),
            *[wspec(s) for s in range(N_SEG)],
            pl.BlockSpec((CONV_W, tn), lambda bi, ti, j: (0, j)),
            tail,
            *extra_specs,
        ],
        out_specs=(attn_act, kf_act, act, act, attn_act, act, act, act, tails),
        scratch_shapes=[pltpu.VMEM((bt * ts, d), BF16),
                        pltpu.VMEM((nj, bt, CONV_W - 1, tn), F32)],
        input_output_aliases=aliases,
        compiler_params=_cparams(3),
        name=name,
    )(x, ada, *([w_in] * N_SEG), conv_w, conv_past, *extra_in)


def _stack_q(q):
    lane = lax.broadcasted_iota(jnp.int32, q.shape, 1)
    zero = jnp.zeros_like(q)
    return jnp.concatenate([jnp.where(lane < HEAD_DIM, q, zero),
                            jnp.where(lane >= HEAD_DIM, q, zero)], axis=0)


def _lam(lq1, lk1, lq2, lk2, lam_init):
    s1 = jnp.sum(lq1[...] * lk1[...], axis=-1, keepdims=True)
    s2 = jnp.sum(lq2[...] * lk2[...], axis=-1, keepdims=True)
    return jnp.exp(s1) - jnp.exp(s2) + lam_init


def _finish_heads(o1, o2, lam, g, lam_init):
    o = o1 - lam * o2
    o = o * lax.rsqrt(jnp.mean(o * o, axis=-1, keepdims=True) + LN_EPS)
    return o * g * (1.0 - lam_init)


def _pattn_kernel(qt_ref, k_ref, vt_ref, vis_a, vis_b, lq1, lk1, lq2, lk2, g_ref, o_ref,
                  qs_scr, acc_scr, sa_scr, sb_scr, *, tq, tk, lam_init):
    qi = pl.program_id(2)
    half = tq // 2
    qt = qt_ref[...]
    feat = lax.broadcasted_iota(jnp.int32, qt.shape, 0)
    zero = jnp.zeros_like(qt)
    qs_scr[:, :tq] = jnp.where(feat < HEAD_DIM, qt, zero)
    qs_scr[:, tq:] = jnp.where(feat >= HEAD_DIM, qt, zero)
    acc_scr[...] = jnp.zeros(acc_scr.shape, F32)
    ones = jnp.ones((ONES_ROWS, tk), BF16)

    def key_start(j):
        return pl.multiple_of(j * tk, tk)

    def scores(j, dst):
        dst[...] = _dot(k_ref[pl.ds(key_start(j), tk), :], qs_scr[...])

    def softmax_pv(j, s, m_old):
        m_new = jnp.maximum(m_old, jnp.max(s, axis=0, keepdims=True))
        a = jnp.exp2(m_old - m_new)
        p = jnp.exp2(s - m_new).astype(BF16)
        vt1 = jnp.concatenate([vt_ref[:, pl.ds(key_start(j), tk)], ones], axis=0)
        return m_new, a, _dot(vt1, p)

    def update(j, s, m_old):
        m_new, a, pv = softmax_pv(j, s, m_old)
        acc_scr[...] = a * acc_scr[...] + pv
        return m_new

    m = jnp.full((1, 2 * tq), NEG_INF, F32)
    npair = (qi * tq) // (2 * tk)
    scores(0, sa_scr)

    def body(i, m):
        j = 2 * i
        scores(j + 1, sb_scr)
        m = update(j, sa_scr[...], m)
        scores(j + 2, sa_scr)
        return update(j + 1, sb_scr[...], m)

    m = lax.fori_loop(0, npair, body, m)
    j = 2 * npair
    late = lambda a: jnp.concatenate([a[:, half:tq], a[:, tq + half:]], axis=1)
    sb_scr[:, :tq] = _dot(k_ref[pl.ds(key_start(j + 1), tk), :], late(qs_scr[...]))
    m = update(j, jnp.where(vis_a[...] > 0.5, sa_scr[...], NEG_INF), m)
    _, a, pv = softmax_pv(j + 1, jnp.where(vis_b[...] > 0.5, sb_scr[:, :tq], NEG_INF), late(m))
    acc_scr[:, half:tq] = a[:, :half] * acc_scr[:, half:tq] + pv[:, :half]
    acc_scr[:, tq + half:] = a[:, half:] * acc_scr[:, tq + half:] + pv[:, half:]

    o = acc_scr[:V_DIM, :] / acc_scr[V_DIM:V_DIM + 1, :]
    lam = _lam(lq1, lk1, lq2, lk2, lam_init)
    o = o[:, :tq] - lam * o[:, tq:]
    o = o * lax.rsqrt(jnp.mean(o * o, axis=0, keepdims=True) + LN_EPS)
    o_ref[...] = (o.T * g_ref[...] * (1.0 - lam_init)).astype(o_ref.dtype)


def _pattn_call(qt, k, vt, lq1, lk1, lq2, lk2, g, *, tq, lam_init, name):
    b, t, w = k.shape
    nh = w // HEAD_W
    tk = tq // 2
    assert tk % CHUNK == 0
    kchunk = lax.broadcasted_iota(jnp.int32, (tk, 2 * tq), 0) // CHUNK
    col = lax.broadcasted_iota(jnp.int32, (tk, 2 * tq), 1)
    vis_a = (kchunk <= (col % tq) // CHUNK).astype(F32)
    vis_b = (kchunk[:, :tq] <= (col[:, :tq] % tk) // CHUNK).astype(F32)
    vec = lambda n: pl.BlockSpec((1, n), lambda bi, h, qi: (0, 0))
    return pl.pallas_call(
        functools.partial(_pattn_kernel, tq=tq, tk=tk, lam_init=lam_init),
        out_shape=jax.ShapeDtypeStruct((b, t, w), BF16),
        grid=(b, nh, t // tq),
        in_specs=[
            pl.BlockSpec((None, HEAD_W, tq), lambda bi, h, qi: (bi, h, qi)),
            pl.BlockSpec((None, t, HEAD_W), lambda bi, h, qi: (bi, 0, h)),
            pl.BlockSpec((None, HEAD_W, t), lambda bi, h, qi: (bi, h, 0)),
            pl.BlockSpec((tk, 2 * tq), lambda bi, h, qi: (0, 0)),
            pl.BlockSpec((tk, tq), lambda bi, h, qi: (0, 0)),
            vec(HEAD_DIM), vec(HEAD_DIM), vec(HEAD_DIM), vec(HEAD_DIM), vec(V_DIM),
        ],
        out_specs=pl.BlockSpec((None, tq, HEAD_W), lambda bi, h, qi: (bi, qi, h)),
        scratch_shapes=[pltpu.VMEM((HEAD_W, 2 * tq), BF16),
                        pltpu.VMEM((V_DIM + ONES_ROWS, 2 * tq), F32),
                        pltpu.VMEM((tk, 2 * tq), F32),
                        pltpu.VMEM((tk, 2 * tq), F32)],
        compiler_params=_cparams(3),
        name=name,
    )(qt, k, vt, vis_a, vis_b, lq1, lk1, lq2, lk2, g)


def _sattn_kernel(q_ref, kt_ref, vc_ref, kn_ref, vn_ref, lq1, lk1, lq2, lk2, g_ref, o_ref,
                  qs_scr, m_scr, l_scr, acc_scr, s_scr, *, lam_init):
    c = pl.program_id(1)
    ts = q_ref.shape[0]
    nh = acc_scr.shape[0]

    def hs(h):
        return slice(h * HEAD_W, (h + 1) * HEAD_W)

    @pl.when(c == 0)
    def _():
        for h in range(nh):
            qs = _stack_q(q_ref[:, hs(h)])
            qs_scr[h] = qs
            s = _dot_nt(qs, kn_ref[:, hs(h)])
            m = jnp.max(s, axis=-1, keepdims=True)
            p = jnp.exp2(s - m)
            m_scr[h] = m
            l_scr[h] = jnp.sum(p, axis=-1, keepdims=True)
            acc_scr[h] = _dot(p.astype(BF16), vn_ref[:, hs(h)])

    pc = vc_ref.shape[0] // nh

    def scores(h):
        s_scr[h % 2] = _dot(qs_scr[h], kt_ref[hs(h), :].astype(BF16))

    scores(0)
    for h in range(nh):
        if h + 1 < nh:
            scores(h + 1)
        s = s_scr[h % 2]
        m_old = m_scr[h]
        m_new = jnp.maximum(m_old, jnp.max(s, axis=-1, keepdims=True))
        a = jnp.exp2(m_old - m_new)
        p = jnp.exp2(s - m_new)
        m_scr[h] = m_new
        l_scr[h] = a * l_scr[h] + jnp.sum(p, axis=-1, keepdims=True)
        v_h = vc_ref[pl.ds(h, pc, stride=nh), :].astype(BF16)
        acc_scr[h] = a * acc_scr[h] + _dot(p.astype(BF16), v_h)

    @pl.when(c == pl.num_programs(1) - 1)
    def _():
        lam = _lam(lq1, lk1, lq2, lk2, lam_init)
        for h in range(nh):
            o = acc_scr[h] / l_scr[h]
            o_ref[:, hs(h)] = _finish_heads(o[:ts], o[ts:], lam, g_ref[...], lam_init).astype(o_ref.dtype)


def _sattn_call(q, cache_kt, cache_v, k_new, v_new, lq1, lk1, lq2, lk2, g, *, layer, pc, lam_init, name):
    b, t, w = q.shape
    p = cache_kt.shape[3]
    nh = w // HEAD_W
    new = pl.BlockSpec((None, t, w), lambda bi, c: (bi, 0, 0))
    vec = lambda n: pl.BlockSpec((1, n), lambda bi, c: (0, 0))
    return pl.pallas_call(
        functools.partial(_sattn_kernel, lam_init=lam_init),
        out_shape=jax.ShapeDtypeStruct((b, t, w), BF16),
        grid=(b, p // pc),
        in_specs=[new,
                  pl.BlockSpec((None, None, w, pc), lambda bi, c: (layer, bi, 0, c)),
                  pl.BlockSpec((None, None, pc * nh, V_DIM), lambda bi, c: (layer, bi, c, 0)),
                  new, new,
                  vec(HEAD_DIM), vec(HEAD_DIM), vec(HEAD_DIM), vec(HEAD_DIM), vec(V_DIM)],
        out_specs=new,
        scratch_shapes=[pltpu.VMEM((nh, 2 * t, HEAD_W), BF16),
                        pltpu.VMEM((nh, 2 * t, 1), F32),
                        pltpu.VMEM((nh, 2 * t, 1), F32),
                        pltpu.VMEM((nh, 2 * t, V_DIM), F32),
                        pltpu.VMEM((2, 2 * t, pc), F32)],
        compiler_params=_cparams(2),
        name=name,
    )(q, cache_kt, cache_v, k_new, v_new, lq1, lk1, lq2, lk2, g)


def _mix_kernel(x_ref, ada_ref, on_ref, yb_ref, sga_ref, sgb_ref, wa_ref, wc_ref, wo_ref,
                g_ref, b_ref, out_ref, *, bt, ts, alpha):
    d = x_ref.shape[-1]
    m = bt * ts
    ya = _dot(on_ref[...].reshape(m, d), wa_ref[...])
    yb = _dot(yb_ref[...].reshape(m, d), wc_ref[...])
    merged = (sga_ref[...].reshape(m, d).astype(F32) * ya
              + sgb_ref[...].reshape(m, d).astype(F32) * yb)
    mix = _dot(merged.astype(BF16), wo_ref[...]).reshape(bt, ts, d)
    y = _ln(alpha * x_ref[...] + ada_ref[:, 2:3, :] * mix)
    out_ref[...] = y * g_ref[...].reshape(1, 1, d) + b_ref[...].reshape(1, 1, d)


def _mix_call(x, ada, on, yb, sga, sgb, wa, wc, wo, g, bln, *, bt, ts, alpha, name):
    b, t, d = x.shape
    act = pl.BlockSpec((bt, ts, d), lambda bi, ti: (bi, ti, 0))
    wsp = pl.BlockSpec((d, d), lambda bi, ti: (0, 0))
    vec = pl.BlockSpec((1, d), lambda bi, ti: (0, 0))
    return pl.pallas_call(
        functools.partial(_mix_kernel, bt=bt, ts=ts, alpha=alpha),
        out_shape=jax.ShapeDtypeStruct((b, t, d), F32),
        grid=(b // bt, t // ts),
        in_specs=[act, pl.BlockSpec((bt, 6, d), lambda bi, ti: (bi, 0, 0)),
                  act, act, act, act, wsp, wsp, wsp, vec, vec],
        out_specs=act,
        compiler_params=_cparams(2),
        name=name,
    )(x, ada, on, yb, sga, sgb, wa, wc, wo, g, bln)


def _ffn_kernel(x_ref, ada_ref, w1_ref, b1_ref, w2_ref, b2_ref, g_ref, b_ref, out_ref,
                u_scr, acc_scr, *, bt, ts, alpha):
    f = pl.program_id(2)
    d = x_ref.shape[-1]
    m = bt * ts

    @pl.when(f == 0)
    def _():
        u = _ln(x_ref[...]) * (1.0 + ada_ref[:, 4:5, :]) + ada_ref[:, 3:4, :]
        u_scr[...] = u.reshape(m, d).astype(BF16)
        acc_scr[...] = jnp.zeros(acc_scr.shape, F32)

    hid = jnp.maximum(_dot(u_scr[...], w1_ref[...]) + b1_ref[...], 0.0)
    acc_scr[...] += _dot((hid * hid).astype(BF16), w2_ref[...])

    @pl.when(f == pl.num_programs(2) - 1)
    def _():
        ff = (acc_scr[...] + b2_ref[...]).reshape(bt, ts, d)
        y = _ln(alpha * x_ref[...] + ada_ref[:, 5:6, :] * ff)
        out_ref[...] = y * g_ref[...].reshape(1, 1, d) + b_ref[...].reshape(1, 1, d)


def _ffn_call(x, ada, w1, b1, w2, b2, g, bln, *, bt, ts, tf, alpha, name):
    b, t, d = x.shape
    dff = w1.shape[1]
    act = pl.BlockSpec((bt, ts, d), lambda bi, ti, f: (bi, ti, 0))
    vec = pl.BlockSpec((1, d), lambda bi, ti, f: (0, 0))
    return pl.pallas_call(
        functools.partial(_ffn_kernel, bt=bt, ts=ts, alpha=alpha),
        out_shape=jax.ShapeDtypeStruct((b, t, d), F32),
        grid=(b // bt, t // ts, dff // tf),
        in_specs=[act, pl.BlockSpec((bt, 6, d), lambda bi, ti, f: (bi, 0, 0)),
                  pl.BlockSpec((d, tf), lambda bi, ti, f: (0, f)),
                  pl.BlockSpec((1, tf), lambda bi, ti, f: (0, f)),
                  pl.BlockSpec((tf, d), lambda bi, ti, f: (f, 0)),
                  vec, vec, vec],
        out_specs=act,
        scratch_shapes=[pltpu.VMEM((bt * ts, d), BF16), pltpu.VMEM((bt * ts, d), F32)],
        compiler_params=_cparams(3),
        name=name,
    )(x, ada, w1, b1, w2, b2, g, bln)


def _row_tiles(b, t, rows):
    if t >= rows:
        ts = rows
        while t % ts:
            ts //= 2
        return 1, ts
    bt = max(1, min(b, rows // t))
    while b % bt:
        bt -= 1
    return bt, t


def _div_tile(n, want):
    tile = min(n, want)
    while n % tile:
        tile //= 2
    return tile


def kernel(x_prompt, x_sample, c_prompt, c_sample, cache_k, cache_v, state_conv, w_in, lam_q1, lam_k1, lam_q2, lam_k2, subln_g, w_attn_out, conv_w, w_conv_out, w_out, w_ada, b_ada, ln1_g, ln1_b, ln2_g, ln2_b, w_ff1, b_ff1, w_ff2, b_ff2):
    n_layers, d, _ = w_in.shape
    bp, tp, _ = x_prompt.shape
    bs, tsamp, _ = x_sample.shape
    past = cache_k.shape[2]
    w = N_HEADS * HEAD_W
    alpha = (2.0 * n_layers) ** 0.25

    w_in_b = w_in.astype(BF16)
    w_ao_b = w_attn_out.astype(BF16)
    w_co_b = w_conv_out.astype(BF16)
    w_o_b = w_out.astype(BF16)
    w_ada_b = w_ada.astype(BF16)
    w1_b = w_ff1.astype(BF16)
    w2_b = w_ff2.astype(BF16)

    c_all = jnp.concatenate([c_prompt, c_sample], axis=0)
    ada_all = _ada_call(c_all, w_ada_b, b_ada[:, None, :]).reshape(n_layers, bp + bs, 6, d)

    cache_kt = jnp.transpose(cache_k, (0, 1, 3, 4, 5, 2)).reshape(n_layers, bs, w, past)
    cache_v2 = cache_v.reshape(n_layers, bs, past * N_HEADS, V_DIM)
    pc = _div_tile(past, 1024)
    zero_past = jnp.zeros((bp, CONV_W - 1, d), F32)

    bt_p, ts_p = _row_tiles(bp, tp, 1024)
    bt_s, ts_s = _row_tiles(bs, tsamp, 1024)
    mbt_p, mts_p = _row_tiles(bp, tp, 512)
    mbt_s, mts_s = _row_tiles(bs, tsamp, 512)
    tn = _div_tile(w, 256)
    tf = _div_tile(w_ff1.shape[2], 512)
    tq = _div_tile(tp, 1024)

    xp, xs = x_prompt, x_sample
    k_stack = lax.empty((n_layers, bp, w, tp), F32)
    outs = [[] for _ in range(6)]
    for layer in range(n_layers):
        lam_init = 0.8 - 0.6 * math.exp(-0.3 * layer)
        ada_p = ada_all[layer, :bp]
        ada_s = ada_all[layer, bp:]
        lams = (lam_q1[layer][None], lam_k1[layer][None], lam_q2[layer][None], lam_k2[layer][None],
                subln_g[layer][None])
        ln1 = (ln1_g[layer][None], ln1_b[layer][None])
        ffn_w = (w1_b[layer], b_ff1[layer][None], w2_b[layer], b_ff2[layer][None],
                 ln2_g[layer][None], ln2_b[layer][None])

        q, k_stack, kb, vf, vb, yb, sga, sgb, nc = _proj_call(
            xp, ada_p, w_in_b[layer], conv_w[layer], zero_past,
            bt=bt_p, ts=ts_p, tn=tn, k_stack=k_stack, layer=layer, name=f"proj_p{layer}")
        on = _pattn_call(q, kb, vb, *lams, tq=tq, lam_init=lam_init, name=f"attn_p{layer}")
        xp = _mix_call(xp, ada_p, on, yb, sga, sgb, w_ao_b[layer], w_co_b[layer], w_o_b[layer], *ln1,
                       bt=mbt_p, ts=mts_p, alpha=alpha, name=f"mix_p{layer}")
        xp = _ffn_call(xp, ada_p, *ffn_w, bt=bt_p, ts=ts_p, tf=tf, alpha=alpha, name=f"ffn_p{layer}")
        outs[1].append(vf)
        outs[2].append(nc[:, -1])

        q, kf, kb, vf, vb, yb, sga, sgb, nc = _proj_call(
            xs, ada_s, w_in_b[layer], conv_w[layer], state_conv[layer],
            bt=bt_s, ts=ts_s, tn=tn, name=f"proj_s{layer}")
        on = _sattn_call(q, cache_kt, cache_v2, kb, vb, *lams, layer=layer, pc=pc, lam_init=lam_init,
                         name=f"attn_s{layer}")
        xs = _mix_call(xs, ada_s, on, yb, sga, sgb, w_ao_b[layer], w_co_b[layer], w_o_b[layer], *ln1,
                       bt=mbt_s, ts=mts_s, alpha=alpha, name=f"mix_s{layer}")
        xs = _ffn_call(xs, ada_s, *ffn_w, bt=bt_s, ts=ts_s, tf=tf, alpha=alpha, name=f"ffn_s{layer}")
        outs[3].append(kf)
        outs[4].append(vf)
        outs[5].append(nc[:, -1])

    new_k_p = jnp.transpose(k_stack.reshape(n_layers, bp, N_HEADS, 2, HEAD_DIM, tp), (0, 1, 5, 2, 3, 4))
    new_v_p = jnp.stack(outs[1]).reshape(n_layers, bp, tp, N_HEADS, V_DIM)
    new_c_p = jnp.stack(outs[2])
    new_k_s = jnp.stack(outs[3]).reshape(n_layers, bs, tsamp, N_HEADS, 2, HEAD_DIM)
    new_v_s = jnp.stack(outs[4]).reshape(n_layers, bs, tsamp, N_HEADS, V_DIM)
    new_c_s = jnp.stack(outs[5])
    return (xp, xs, new_k_p, new_v_p, new_c_p, new_k_s, new_v_s, new_c_s)
```

```python
import functools
import math

import jax
import jax.numpy as jnp
from jax import lax
from jax.experimental import pallas as pl
from jax.experimental.pallas import tpu as pltpu

N_HEADS = 8
HEAD_DIM = 64
V_DIM = 2 * HEAD_DIM
HEAD_W = 2 * HEAD_DIM
CHUNK = 64
CONV_W = 3
LN_EPS = 1e-5
NEG_INF = -1e30
N_SEG = 8
Q_SCALE = (HEAD_DIM ** -0.5) * math.log2(math.e)
ONES_ROWS = 16

F32 = jnp.float32
BF16 = jnp.bfloat16

VMEM_LIMIT = 56 * 1024 * 1024


def _cparams(n_axes):
    return pltpu.CompilerParams(dimension_semantics=("arbitrary",) * n_axes,
                                vmem_limit_bytes=VMEM_LIMIT)


def _ln(x):
    mu = jnp.mean(x, axis=-1, keepdims=True)
    xc = x - mu
    var = jnp.mean(xc * xc, axis=-1, keepdims=True)
    return xc * lax.rsqrt(var + LN_EPS)


def _dot(a, b):
    return jnp.dot(a, b, preferred_element_type=F32)


def _dot_nt(a, b):
    return lax.dot_general(a, b, (((1,), (1,)), ((), ())), preferred_element_type=F32)


def _ada_kernel(c_ref, w_ref, b_ref, o_ref):
    c = c_ref[...]
    sc = c * jax.nn.sigmoid(c)
    o_ref[...] = _dot(sc.astype(BF16), w_ref[...]) + b_ref[...]


def _ada_call(c_all, w_ada, b_ada):
    n_layers, d, d6 = w_ada.shape
    nb = c_all.shape[0]
    tn = d
    return pl.pallas_call(
        _ada_kernel,
        out_shape=jax.ShapeDtypeStruct((n_layers, nb, d6), F32),
        grid=(n_layers, d6 // tn),
        in_specs=[
            pl.BlockSpec((nb, d), lambda l, j: (0, 0)),
            pl.BlockSpec((None, d, tn), lambda l, j: (l, 0, j)),
            pl.BlockSpec((None, 1, tn), lambda l, j: (l, 0, j)),
        ],
        out_specs=pl.BlockSpec((None, nb, tn), lambda l, j: (l, 0, j)),
        compiler_params=_cparams(2),
        name="ada",
    )(c_all, w_ada, b_ada)


def _proj_kernel(x_ref, ada_ref, wq, wk, wv, wgb, wgc, wh, wga, wgg, cw_ref, past_ref, *rest,
                 bt, ts, feature_major):
    if feature_major:
        rest = rest[1:]
    (q_ref, kf_ref, kb_ref, vf_ref, vb_ref, yb_ref, sga_ref, sgb_ref, nc_ref, u_scr, carry_scr) = rest
    t = pl.program_id(1)
    j = pl.program_id(2)
    d = x_ref.shape[-1]
    tn = wq.shape[-1]
    m = bt * ts

    def attn_layout(a):
        if feature_major:
            return a.T.astype(BF16).reshape(bt, tn, ts)
        return a.astype(BF16).reshape(bt, ts, tn)

    @pl.when(j == 0)
    def _():
        xn = _ln(x_ref[...])
        u = xn * (1.0 + ada_ref[:, 1:2, :]) + ada_ref[:, 0:1, :]
        u_scr[...] = u.reshape(m, d).astype(BF16)

    @pl.when(t == 0)
    def _():
        carry_scr[j] = past_ref[...]

    u = u_scr[...]

    def seg(w_ref):
        return _dot(u, w_ref[...])

    q_ref[...] = attn_layout(seg(wq) * Q_SCALE)
    k = seg(wk)
    kf_ref[...] = k.T.reshape(bt, tn, ts) if feature_major else k.reshape(bt, ts, tn)
    kb_ref[...] = k.astype(BF16).reshape(bt, ts, tn)
    v = seg(wv)
    vf_ref[...] = v.reshape(bt, ts, tn)
    vb_ref[...] = attn_layout(v)

    z2 = seg(wgc) * seg(wh)
    z = z2.reshape(bt, ts, tn)
    prev = carry_scr[j]
    p0 = prev[:, 0:1, :]
    p1 = prev[:, 1:2, :]
    row = lax.broadcasted_iota(jnp.int32, (bt, ts, tn), 1)
    zr1 = pltpu.roll(z2, 1, 0).reshape(bt, ts, tn)
    zr2 = pltpu.roll(z2, 2, 0).reshape(bt, ts, tn)
    s1 = jnp.where(row == 0, p1, zr1)
    s2 = jnp.where(row == 0, p0, jnp.where(row == 1, p1, zr2))
    cw = cw_ref[...]
    conv = cw[0:1, :].reshape(1, 1, tn) * s2 + cw[1:2, :].reshape(1, 1, tn) * s1
    conv = conv + cw[2:3, :].reshape(1, 1, tn) * z
    tail = z[:, ts - (CONV_W - 1):, :]
    carry_scr[j] = tail
    nc_ref[...] = tail

    yb_ref[...] = (seg(wgb).reshape(bt, ts, tn) * conv).astype(BF16)
    sga_ref[...] = jax.nn.sigmoid(seg(wga)).astype(BF16).reshape(bt, ts, tn)
    sgb_ref[...] = jax.nn.sigmoid(seg(wgg)).astype(BF16).reshape(bt, ts, tn)


def _proj_call(x, ada, w_in, conv_w, conv_past, *, bt, ts, tn, name, k_stack=None, layer=0):
    b, t, d = x.shape
    w = w_in.shape[1] // N_SEG
    nj = w // tn
    grid = (b // bt, t // ts, nj)
    feature_major = k_stack is not None

    def wspec(s):
        return pl.BlockSpec((d, tn), lambda bi, ti, j, s=s: (0, s * nj + j))

    act = pl.BlockSpec((bt, ts, tn), lambda bi, ti, j: (bi, ti, j))
    sds = jax.ShapeDtypeStruct
    extra_in, extra_specs, aliases = (), (), {}
    if feature_major:
        attn_act = pl.BlockSpec((bt, tn, ts), lambda bi, ti, j: (bi, j, ti))
        attn_shape = sds((b, w, t), BF16)
        kf_act = pl.BlockSpec((None, bt, tn, ts), lambda bi, ti, j: (layer, bi, j, ti))
        kf_shape = sds(k_stack.shape, F32)
        extra_in, extra_specs = (k_stack,), (pl.BlockSpec(memory_space=pl.ANY),)
        aliases = {4 + N_SEG: 1}
    else:
        attn_act, attn_shape = act, sds((b, t, w), BF16)
        kf_act, kf_shape = act, sds((b, t, w), F32)
    tail = pl.BlockSpec((bt, CONV_W - 1, tn), lambda bi, ti, j: (bi, 0, j))
    tails = pl.BlockSpec((bt, None, CONV_W - 1, tn), lambda bi, ti, j: (bi, ti, 0, j))
    out_shape = (attn_shape, kf_shape, sds((b, t, w), BF16),
                 sds((b, t, w), F32), attn_shape, sds((b, t, w), BF16),
                 sds((b, t, w), BF16), sds((b, t, w), BF16), sds((b, t // ts, CONV_W - 1, w), F32))
    return pl.pallas_call(
        functools.partial(_proj_kernel, bt=bt, ts=ts, feature_major=feature_major),
        out_shape=out_shape,
        grid=grid,
        in_specs=[
            pl.BlockSpec((bt, ts, d), lambda bi, ti, j: (bi, ti, 0)),
            pl.BlockSpec((bt, 6, d), lambda bi, ti, j: (bi, 0, 0)),
            *[wspec(s) for s in range(N_SEG)],
            pl.BlockSpec((CONV_W, tn), lambda bi, ti, j: (0, j)),
            tail,
            *extra_specs,
        ],
        out_specs=(attn_act, kf_act, act, act, attn_act, act, act, act, tails),
        scratch_shapes=[pltpu.VMEM((bt * ts, d), BF16),
                        pltpu.VMEM((nj, bt, CONV_W - 1, tn), F32)],
        input_output_aliases=aliases,
        compiler_params=_cparams(3),
        name=name,
    )(x, ada, *([w_in] * N_SEG), conv_w, conv_past, *extra_in)


def _stack_q(q):
    lane = lax.broadcasted_iota(jnp.int32, q.shape, 1)
    zero = jnp.zeros_like(q)
    return jnp.concatenate([jnp.where(lane < HEAD_DIM, q, zero),
                            jnp.where(lane >= HEAD_DIM, q, zero)], axis=0)


def _lam(lq1, lk1, lq2, lk2, lam_init):
    s1 = jnp.sum(lq1[...] * lk1[...], axis=-1, keepdims=True)
    s2 = jnp.sum(lq2[...] * lk2[...], axis=-1, keepdims=True)
    return jnp.exp(s1) - jnp.exp(s2) + lam_init


def _finish_heads(o1, o2, lam, g, lam_init):
    o = o1 - lam * o2
    o = o * lax.rsqrt(jnp.mean(o * o, axis=-1, keepdims=True) + LN_EPS)
    return o * g * (1.0 - lam_init)


def _pattn_kernel(qt_ref, k_ref, vt_ref, vis_a, vis_b, lq1, lk1, lq2, lk2, g_ref, o_ref,
                  qs_scr, acc_scr, sa_scr, sb_scr, *, tq, tk, lam_init):
    half = tq // 2
    nq = qt_ref.shape[1] // tq
    ones = jnp.ones((ONES_ROWS, tk), BF16)
    lam = _lam(lq1, lk1, lq2, lk2, lam_init)
    late = lambda a: jnp.concatenate([a[:, half:tq], a[:, tq + half:]], axis=1)

    def key_start(j):
        return j * tk if isinstance(j, int) else pl.multiple_of(j * tk, tk)

    def stack_queries(qi):
        qt = qt_ref[:, qi * tq:(qi + 1) * tq]
        feat = lax.broadcasted_iota(jnp.int32, qt.shape, 0)
        zero = jnp.zeros_like(qt)
        qs_scr[:, :tq] = jnp.where(feat < HEAD_DIM, qt, zero)
        qs_scr[:, tq:] = jnp.where(feat >= HEAD_DIM, qt, zero)

    def scores(j, dst):
        dst[...] = _dot(k_ref[pl.ds(key_start(j), tk), :], qs_scr[...])

    def softmax_pv(j, s, m_old):
        m_new = jnp.maximum(m_old, jnp.max(s, axis=0, keepdims=True))
        a = jnp.exp2(m_old - m_new)
        p = jnp.exp2(s - m_new).astype(BF16)
        vt1 = jnp.concatenate([vt_ref[:, pl.ds(key_start(j), tk)], ones], axis=0)
        return m_new, a, _dot(vt1, p)

    def update(j, s, m_old):
        m_new, a, pv = softmax_pv(j, s, m_old)
        acc_scr[...] = a * acc_scr[...] + pv
        return m_new

    def body(i, m):
        j = 2 * i
        scores(j + 1, sb_scr)
        m = update(j, sa_scr[...], m)
        scores(j + 2, sa_scr)
        return update(j + 1, sb_scr[...], m)

    stack_queries(0)
    scores(0, sa_scr)
    for qi in range(nq):
        acc_scr[...] = jnp.zeros(acc_scr.shape, F32)
        m = jnp.full((1, 2 * tq), NEG_INF, F32)
        npair = (qi * tq) // (2 * tk)
        if npair:
            m = lax.fori_loop(0, npair, body, m)
        j = 2 * npair
        sb_scr[:, :tq] = _dot(k_ref[pl.ds(key_start(j + 1), tk), :], late(qs_scr[...]))
        m = update(j, jnp.where(vis_a[...] > 0.5, sa_scr[...], NEG_INF), m)
        _, a, pv = softmax_pv(j + 1, jnp.where(vis_b[...] > 0.5, sb_scr[:, :tq], NEG_INF), late(m))
        acc_scr[:, half:tq] = a[:, :half] * acc_scr[:, half:tq] + pv[:, :half]
        acc_scr[:, tq + half:] = a[:, half:] * acc_scr[:, tq + half:] + pv[:, half:]
        o = acc_scr[:V_DIM, :] / acc_scr[V_DIM:V_DIM + 1, :]
        if qi + 1 < nq:
            stack_queries(qi + 1)
            scores(0, sa_scr)
        o = o[:, :tq] - lam * o[:, tq:]
        o = o * lax.rsqrt(jnp.mean(o * o, axis=0, keepdims=True) + LN_EPS)
        o_ref[qi * tq:(qi + 1) * tq, :] = (o.T * g_ref[...] * (1.0 - lam_init)).astype(o_ref.dtype)


def _pattn_call(qt, k, vt, lq1, lk1, lq2, lk2, g, *, tq, lam_init, name):
    b, t, w = k.shape
    nh = w // HEAD_W
    tk = tq // 2
    assert tk % CHUNK == 0
    kchunk = lax.broadcasted_iota(jnp.int32, (tk, 2 * tq), 0) // CHUNK
    col = lax.broadcasted_iota(jnp.int32, (tk, 2 * tq), 1)
    vis_a = (kchunk <= (col % tq) // CHUNK).astype(F32)
    vis_b = (kchunk[:, :tq] <= (col[:, :tq] % tk) // CHUNK).astype(F32)
    vec = lambda n: pl.BlockSpec((1, n), lambda bi, h: (0, 0))
    feature_major = pl.BlockSpec((None, HEAD_W, t), lambda bi, h: (bi, h, 0))
    token_major = pl.BlockSpec((None, t, HEAD_W), lambda bi, h: (bi, 0, h))
    return pl.pallas_call(
        functools.partial(_pattn_kernel, tq=tq, tk=tk, lam_init=lam_init),
        out_shape=jax.ShapeDtypeStruct((b, t, w), BF16),
        grid=(b, nh),
        in_specs=[
            feature_major, token_major, feature_major,
            pl.BlockSpec((tk, 2 * tq), lambda bi, h: (0, 0)),
            pl.BlockSpec((tk, tq), lambda bi, h: (0, 0)),
            vec(HEAD_DIM), vec(HEAD_DIM), vec(HEAD_DIM), vec(HEAD_DIM), vec(V_DIM),
        ],
        out_specs=token_major,
        scratch_shapes=[pltpu.VMEM((HEAD_W, 2 * tq), BF16),
                        pltpu.VMEM((V_DIM + ONES_ROWS, 2 * tq), F32),
                        pltpu.VMEM((tk, 2 * tq), F32),
                        pltpu.VMEM((tk, 2 * tq), F32)],
        compiler_params=_cparams(2),
        name=name,
    )(qt, k, vt, vis_a, vis_b, lq1, lk1, lq2, lk2, g)


def _sattn_kernel(q_ref, kt_ref, vc_ref, kn_ref, vn_ref, lq1, lk1, lq2, lk2, g_ref, o_ref,
                  qs_scr, m_scr, l_scr, acc_scr, s_scr, *, lam_init):
    c = pl.program_id(1)
    ts = q_ref.shape[0]
    nh = acc_scr.shape[0]

    def hs(h):
        return slice(h * HEAD_W, (h + 1) * HEAD_W)

    @pl.when(c == 0)
    def _():
        for h in range(nh):
            qs = _stack_q(q_ref[:, hs(h)])
            qs_scr[h] = qs
            s = _dot_nt(qs, kn_ref[:, hs(h)])
            m = jnp.max(s, axis=-1, keepdims=True)
            p = jnp.exp2(s - m)
            m_scr[h] = m
            l_scr[h] = jnp.sum(p, axis=-1, keepdims=True)
            acc_scr[h] = _dot(p.astype(BF16), vn_ref[:, hs(h)])

    pc = vc_ref.shape[0] // nh

    def scores(h):
        s_scr[h % 2] = _dot(qs_scr[h], kt_ref[hs(h), :].astype(BF16))

    scores(0)
    for h in range(nh):
        if h + 1 < nh:
            scores(h + 1)
        s = s_scr[h % 2]
        m_old = m_scr[h]
        m_new = jnp.maximum(m_old, jnp.max(s, axis=-1, keepdims=True))
        a = jnp.exp2(m_old - m_new)
        p = jnp.exp2(s - m_new)
        m_scr[h] = m_new
        l_scr[h] = a * l_scr[h] + jnp.sum(p, axis=-1, keepdims=True)
        v_h = vc_ref[pl.ds(h, pc, stride=nh), :].astype(BF16)
        acc_scr[h] = a * acc_scr[h] + _dot(p.astype(BF16), v_h)

    @pl.when(c == pl.num_programs(1) - 1)
    def _():
        lam = _lam(lq1, lk1, lq2, lk2, lam_init)
        for h in range(nh):
            o = acc_scr[h] / l_scr[h]
            o_ref[:, hs(h)] = _finish_heads(o[:ts], o[ts:], lam, g_ref[...], lam_init).astype(o_ref.dtype)


def _sattn_call(q, cache_kt, cache_v, k_new, v_new, lq1, lk1, lq2, lk2, g, *, layer, pc, lam_init, name):
    b, t, w = q.shape
    p = cache_kt.shape[3]
    nh = w // HEAD_W
    new = pl.BlockSpec((None, t, w), lambda bi, c: (bi, 0, 0))
    vec = lambda n: pl.BlockSpec((1, n), lambda bi, c: (0, 0))
    return pl.pallas_call(
        functools.partial(_sattn_kernel, lam_init=lam_init),
        out_shape=jax.ShapeDtypeStruct((b, t, w), BF16),
        grid=(b, p // pc),
        in_specs=[new,
                  pl.BlockSpec((None, None, w, pc), lambda bi, c: (layer, bi, 0, c)),
                  pl.BlockSpec((None, None, pc * nh, V_DIM), lambda bi, c: (layer, bi, c, 0)),
                  new, new,
                  vec(HEAD_DIM), vec(HEAD_DIM), vec(HEAD_DIM), vec(HEAD_DIM), vec(V_DIM)],
        out_specs=new,
        scratch_shapes=[pltpu.VMEM((nh, 2 * t, HEAD_W), BF16),
                        pltpu.VMEM((nh, 2 * t, 1), F32),
                        pltpu.VMEM((nh, 2 * t, 1), F32),
                        pltpu.VMEM((nh, 2 * t, V_DIM), F32),
                        pltpu.VMEM((2, 2 * t, pc), F32)],
        compiler_params=_cparams(2),
        name=name,
    )(q, cache_kt, cache_v, k_new, v_new, lq1, lk1, lq2, lk2, g)


def _mix_kernel(x_ref, ada_ref, on_ref, yb_ref, sga_ref, sgb_ref, wa_ref, wc_ref, wo_ref,
                g_ref, b_ref, out_ref, *, bt, ts, alpha):
    d = x_ref.shape[-1]
    m = bt * ts
    ya = _dot(on_ref[...].reshape(m, d), wa_ref[...])
    yb = _dot(yb_ref[...].reshape(m, d), wc_ref[...])
    merged = (sga_ref[...].reshape(m, d).astype(F32) * ya
              + sgb_ref[...].reshape(m, d).astype(F32) * yb)
    mix = _dot(merged.astype(BF16), wo_ref[...]).reshape(bt, ts, d)
    y = _ln(alpha * x_ref[...] + ada_ref[:, 2:3, :] * mix)
    out_ref[...] = y * g_ref[...].reshape(1, 1, d) + b_ref[...].reshape(1, 1, d)


def _mix_call(x, ada, on, yb, sga, sgb, wa, wc, wo, g, bln, *, bt, ts, alpha, name):
    b, t, d = x.shape
    act = pl.BlockSpec((bt, ts, d), lambda bi, ti: (bi, ti, 0))
    wsp = pl.BlockSpec((d, d), lambda bi, ti: (0, 0))
    vec = pl.BlockSpec((1, d), lambda bi, ti: (0, 0))
    return pl.pallas_call(
        functools.partial(_mix_kernel, bt=bt, ts=ts, alpha=alpha),
        out_shape=jax.ShapeDtypeStruct((b, t, d), F32),
        grid=(b // bt, t // ts),
        in_specs=[act, pl.BlockSpec((bt, 6, d), lambda bi, ti: (bi, 0, 0)),
                  act, act, act, act, wsp, wsp, wsp, vec, vec],
        out_specs=act,
        compiler_params=_cparams(2),
        name=name,
    )(x, ada, on, yb, sga, sgb, wa, wc, wo, g, bln)


def _ffn_kernel(x_ref, ada_ref, w1_ref, b1_ref, w2_ref, b2_ref, g_ref, b_ref, out_ref,
                u_scr, acc_scr, *, bt, ts, alpha):
    f = pl.program_id(2)
    d = x_ref.shape[-1]
    m = bt * ts

    @pl.when(f == 0)
    def _():
        u = _ln(x_ref[...]) * (1.0 + ada_ref[:, 4:5, :]) + ada_ref[:, 3:4, :]
        u_scr[...] = u.reshape(m, d).astype(BF16)
        acc_scr[...] = jnp.zeros(acc_scr.shape, F32)

    hid = jnp.maximum(_dot(u_scr[...], w1_ref[...]) + b1_ref[...], 0.0)
    acc_scr[...] += _dot((hid * hid).astype(BF16), w2_ref[...])

    @pl.when(f == pl.num_programs(2) - 1)
    def _():
        ff = (acc_scr[...] + b2_ref[...]).reshape(bt, ts, d)
        y = _ln(alpha * x_ref[...] + ada_ref[:, 5:6, :] * ff)
        out_ref[...] = y * g_ref[...].reshape(1, 1, d) + b_ref[...].reshape(1, 1, d)


def _ffn_call(x, ada, w1, b1, w2, b2, g, bln, *, bt, ts, tf, alpha, name):
    b, t, d = x.shape
    dff = w1.shape[1]
    act = pl.BlockSpec((bt, ts, d), lambda bi, ti, f: (bi, ti, 0))
    vec = pl.BlockSpec((1, d), lambda bi, ti, f: (0, 0))
    return pl.pallas_call(
        functools.partial(_ffn_kernel, bt=bt, ts=ts, alpha=alpha),
        out_shape=jax.ShapeDtypeStruct((b, t, d), F32),
        grid=(b // bt, t // ts, dff // tf),
        in_specs=[act, pl.BlockSpec((bt, 6, d), lambda bi, ti, f: (bi, 0, 0)),
                  pl.BlockSpec((d, tf), lambda bi, ti, f: (0, f)),
                  pl.BlockSpec((1, tf), lambda bi, ti, f: (0, f)),
                  pl.BlockSpec((tf, d), lambda bi, ti, f: (f, 0)),
                  vec, vec, vec],
        out_specs=act,
        scratch_shapes=[pltpu.VMEM((bt * ts, d), BF16), pltpu.VMEM((bt * ts, d), F32)],
        compiler_params=_cparams(3),
        name=name,
    )(x, ada, w1, b1, w2, b2, g, bln)


def _row_tiles(b, t, rows):
    if t >= rows:
        ts = rows
        while t % ts:
            ts //= 2
        return 1, ts
    bt = max(1, min(b, rows // t))
    while b % bt:
        bt -= 1
    return bt, t


def _div_tile(n, want):
    tile = min(n, want)
    while n % tile:
        tile //= 2
    return tile


def kernel(x_prompt, x_sample, c_prompt, c_sample, cache_k, cache_v, state_conv, w_in, lam_q1, lam_k1, lam_q2, lam_k2, subln_g, w_attn_out, conv_w, w_conv_out, w_out, w_ada, b_ada, ln1_g, ln1_b, ln2_g, ln2_b, w_ff1, b_ff1, w_ff2, b_ff2):
    n_layers, d, _ = w_in.shape
    bp, tp, _ = x_prompt.shape
    bs, tsamp, _ = x_sample.shape
    past = cache_k.shape[2]
    w = N_HEADS * HEAD_W
    alpha = (2.0 * n_layers) ** 0.25

    w_in_b = w_in.astype(BF16)
    w_ao_b = w_attn_out.astype(BF16)
    w_co_b = w_conv_out.astype(BF16)
    w_o_b = w_out.astype(BF16)
    w_ada_b = w_ada.astype(BF16)
    w1_b = w_ff1.astype(BF16)
    w2_b = w_ff2.astype(BF16)

    c_all = jnp.concatenate([c_prompt, c_sample], axis=0)
    ada_all = _ada_call(c_all, w_ada_b, b_ada[:, None, :]).reshape(n_layers, bp + bs, 6, d)

    cache_kt = jnp.transpose(cache_k, (0, 1, 3, 4, 5, 2)).reshape(n_layers, bs, w, past)
    cache_v2 = cache_v.reshape(n_layers, bs, past * N_HEADS, V_DIM)
    pc = _div_tile(past, 1024)
    zero_past = jnp.zeros((bp, CONV_W - 1, d), F32)

    bt_p, ts_p = _row_tiles(bp, tp, 1024)
    bt_s, ts_s = _row_tiles(bs, tsamp, 1024)
    mbt_p, mts_p = _row_tiles(bp, tp, 512)
    mbt_s, mts_s = _row_tiles(bs, tsamp, 512)
    tn = _div_tile(w, 256)
    tf = _div_tile(w_ff1.shape[2], 512)
    tq = _div_tile(tp, 1024)

    xp, xs = x_prompt, x_sample
    k_stack = lax.empty((n_layers, bp, w, tp), F32)
    outs = [[] for _ in range(6)]
    for layer in range(n_layers):
        lam_init = 0.8 - 0.6 * math.exp(-0.3 * layer)
        ada_p = ada_all[layer, :bp]
        ada_s = ada_all[layer, bp:]
        lams = (lam_q1[layer][None], lam_k1[layer][None], lam_q2[layer][None], lam_k2[layer][None],
                subln_g[layer][None])
        ln1 = (ln1_g[layer][None], ln1_b[layer][None])
        ffn_w = (w1_b[layer], b_ff1[layer][None], w2_b[layer], b_ff2[layer][None],
                 ln2_g[layer][None], ln2_b[layer][None])

        q, k_stack, kb, vf, vb, yb, sga, sgb, nc = _proj_call(
            xp, ada_p, w_in_b[layer], conv_w[layer], zero_past,
            bt=bt_p, ts=ts_p, tn=tn, k_stack=k_stack, layer=layer, name=f"proj_p{layer}")
        on = _pattn_call(q, kb, vb, *lams, tq=tq, lam_init=lam_init, name=f"attn_p{layer}")
        xp = _mix_call(xp, ada_p, on, yb, sga, sgb, w_ao_b[layer], w_co_b[layer], w_o_b[layer], *ln1,
                       bt=mbt_p, ts=mts_p, alpha=alpha, name=f"mix_p{layer}")
        xp = _ffn_call(xp, ada_p, *ffn_w, bt=bt_p, ts=ts_p, tf=tf, alpha=alpha, name=f"ffn_p{layer}")
        outs[1].append(vf)
        outs[2].append(nc[:, -1])

        q, kf, kb, vf, vb, yb, sga, sgb, nc = _proj_call(
            xs, ada_s, w_in_b[layer], conv_w[layer], state_conv[layer],
            bt=bt_s, ts=ts_s, tn=tn, name=f"proj_s{layer}")
        on = _sattn_call(q, cache_kt, cache_v2, kb, vb, *lams, layer=layer, pc=pc, lam_init=lam_init,
                         name=f"attn_s{layer}")
        xs = _mix_call(xs, ada_s, on, yb, sga, sgb, w_ao_b[layer], w_co_b[layer], w_o_b[layer], *ln1,
                       bt=mbt_s, ts=mts_s, alpha=alpha, name=f"mix_s{layer}")
        xs = _ffn_call(xs, ada_s, *ffn_w, bt=bt_s, ts=ts_s, tf=tf, alpha=alpha, name=f"ffn_s{layer}")
        outs[3].append(kf)
        outs[4].append(vf)
        outs[5].append(nc[:, -1])

    new_k_p = jnp.transpose(k_stack.reshape(n_layers, bp, N_HEADS, 2, HEAD_DIM, tp), (0, 1, 5, 2, 3, 4))
    new_v_p = jnp.stack(outs[1]).reshape(n_layers, bp, tp, N_HEADS, V_DIM)
    new_c_p = jnp.stack(outs[2])
    new_k_s = jnp.stack(outs[3]).reshape(n_layers, bs, tsamp, N_HEADS, 2, HEAD_DIM)
    new_v_s = jnp.stack(outs[4]).reshape(n_layers, bs, tsamp, N_HEADS, V_DIM)
    new_c_s = jnp.stack(outs[5])
    return (xp, xs, new_k_p, new_v_p, new_c_p, new_k_s, new_v_s, new_c_s)
```

```python
import functools
import math

import jax
import jax.numpy as jnp
from jax import lax
from jax.experimental import pallas as pl
from jax.experimental.pallas import tpu as pltpu

N_HEADS = 8
HEAD_DIM = 64
V_DIM = 2 * HEAD_DIM
HEAD_W = 2 * HEAD_DIM
CHUNK = 64
CONV_W = 3
LN_EPS = 1e-5
NEG_INF = -1e30
N_SEG = 8
Q_SCALE = (HEAD_DIM ** -0.5) * math.log2(math.e)
ONES_ROWS = 16
SCORE_PAD = 128

F32 = jnp.float32
BF16 = jnp.bfloat16

VMEM_LIMIT = 56 * 1024 * 1024


def _cparams(n_axes):
    return pltpu.CompilerParams(dimension_semantics=("arbitrary",) * n_axes,
                                vmem_limit_bytes=VMEM_LIMIT)


def _ln(x):
    mu = jnp.mean(x, axis=-1, keepdims=True)
    xc = x - mu
    var = jnp.mean(xc * xc, axis=-1, keepdims=True)
    return xc * lax.rsqrt(var + LN_EPS)


def _dot(a, b):
    return jnp.dot(a, b, preferred_element_type=F32)


def _dot_nt(a, b):
    return lax.dot_general(a, b, (((1,), (1,)), ((), ())), preferred_element_type=F32)


def _ada_kernel(c_ref, w_ref, b_ref, o_ref):
    c = c_ref[...]
    sc = c * jax.nn.sigmoid(c)
    o_ref[...] = _dot(sc.astype(BF16), w_ref[...]) + b_ref[...]


def _ada_call(c_all, w_ada, b_ada):
    n_layers, d, d6 = w_ada.shape
    nb = c_all.shape[0]
    tn = d
    return pl.pallas_call(
        _ada_kernel,
        out_shape=jax.ShapeDtypeStruct((n_layers, nb, d6), F32),
        grid=(n_layers, d6 // tn),
        in_specs=[
            pl.BlockSpec((nb, d), lambda l, j: (0, 0)),
            pl.BlockSpec((None, d, tn), lambda l, j: (l, 0, j)),
            pl.BlockSpec((None, 1, tn), lambda l, j: (l, 0, j)),
        ],
        out_specs=pl.BlockSpec((None, nb, tn), lambda l, j: (l, 0, j)),
        compiler_params=_cparams(2),
        name="ada",
    )(c_all, w_ada, b_ada)


def _proj_kernel(x_ref, ada_ref, wq, wk, wv, wgb, wgc, wh, wga, wgg, cw_ref, past_ref, *rest,
                 bt, ts, feature_major):
    if feature_major:
        rest = rest[1:]
    (q_ref, kf_ref, kb_ref, vf_ref, vb_ref, yb_ref, sga_ref, sgb_ref, nc_ref, u_scr, carry_scr) = rest
    t = pl.program_id(1)
    j = pl.program_id(2)
    d = x_ref.shape[-1]
    tn = wq.shape[-1]
    m = bt * ts

    def attn_layout(a):
        if feature_major:
            return a.T.astype(BF16).reshape(bt, tn, ts)
        return a.astype(BF16).reshape(bt, ts, tn)

    @pl.when(j == 0)
    def _():
        xn = _ln(x_ref[...])
        u = xn * (1.0 + ada_ref[:, 1:2, :]) + ada_ref[:, 0:1, :]
        u_scr[...] = u.reshape(m, d).astype(BF16)

    @pl.when(t == 0)
    def _():
        carry_scr[j] = past_ref[...]

    u = u_scr[...]

    def seg(w_ref):
        return _dot(u, w_ref[...])

    q_ref[...] = attn_layout(seg(wq) * Q_SCALE)
    k = seg(wk)
    kf_ref[...] = k.T.reshape(bt, tn, ts) if feature_major else k.reshape(bt, ts, tn)
    kb_ref[...] = k.astype(BF16).reshape(bt, ts, tn)
    v = seg(wv)
    vf_ref[...] = v.reshape(bt, ts, tn)
    vb_ref[...] = attn_layout(v)

    z2 = seg(wgc) * seg(wh)
    z = z2.reshape(bt, ts, tn)
    prev = carry_scr[j]
    p0 = prev[:, 0:1, :]
    p1 = prev[:, 1:2, :]
    row = lax.broadcasted_iota(jnp.int32, (bt, ts, tn), 1)
    zr1 = pltpu.roll(z2, 1, 0).reshape(bt, ts, tn)
    zr2 = pltpu.roll(z2, 2, 0).reshape(bt, ts, tn)
    s1 = jnp.where(row == 0, p1, zr1)
    s2 = jnp.where(row == 0, p0, jnp.where(row == 1, p1, zr2))
    cw = cw_ref[...]
    conv = cw[0:1, :].reshape(1, 1, tn) * s2 + cw[1:2, :].reshape(1, 1, tn) * s1
    conv = conv + cw[2:3, :].reshape(1, 1, tn) * z
    tail = z[:, ts - (CONV_W - 1):, :]
    carry_scr[j] = tail
    nc_ref[...] = tail

    yb_ref[...] = (seg(wgb).reshape(bt, ts, tn) * conv).astype(BF16)
    sga_ref[...] = jax.nn.sigmoid(seg(wga)).astype(BF16).reshape(bt, ts, tn)
    sgb_ref[...] = jax.nn.sigmoid(seg(wgg)).astype(BF16).reshape(bt, ts, tn)


def _proj_call(x, ada, w_in, conv_w, conv_past, *, bt, ts, tn, name, k_stack=None, layer=0):
    b, t, d = x.shape
    w = w_in.shape[1] // N_SEG
    nj = w // tn
    grid = (b // bt, t // ts, nj)
    feature_major = k_stack is not None

    def wspec(s):
        return pl.BlockSpec((d, tn), lambda bi, ti, j, s=s: (0, s * nj + j))

    act = pl.BlockSpec((bt, ts, tn), lambda bi, ti, j: (bi, ti, j))
    sds = jax.ShapeDtypeStruct
    extra_in, extra_specs, aliases = (), (), {}
    if feature_major:
        attn_act = pl.BlockSpec((bt, tn, ts), lambda bi, ti, j: (bi, j, ti))
        attn_shape = sds((b, w, t), BF16)
        kf_act = pl.BlockSpec((None, bt, tn, ts), lambda bi, ti, j: (layer, bi, j, ti))
        kf_shape = sds(k_stack.shape, F32)
        extra_in, extra_specs = (k_stack,), (pl.BlockSpec(memory_space=pl.ANY),)
        aliases = {4 + N_SEG: 1}
    else:
        attn_act, attn_shape = act, sds((b, t, w), BF16)
        kf_act, kf_shape = act, sds((b, t, w), F32)
    tail = pl.BlockSpec((bt, CONV_W - 1, tn), lambda bi, ti, j: (bi, 0, j))
    tails = pl.BlockSpec((bt, None, CONV_W - 1, tn), lambda bi, ti, j: (bi, ti, 0, j))
    out_shape = (attn_shape, kf_shape, sds((b, t, w), BF16),
                 sds((b, t, w), F32), attn_shape, sds((b, t, w), BF16),
                 sds((b, t, w), BF16), sds((b, t, w), BF16), sds((b, t // ts, CONV_W - 1, w), F32))
    return pl.pallas_call(
        functools.partial(_proj_kernel, bt=bt, ts=ts, feature_major=feature_major),
        out_shape=out_shape,
        grid=grid,
        in_specs=[
            pl.BlockSpec((bt, ts, d), lambda bi, ti, j: (bi, ti, 0)),
            pl.BlockSpec((bt, 6, d), lambda bi, ti, j: (bi, 0, 0)),
            *[wspec(s) for s in range(N_SEG)],
            pl.BlockSpec((CONV_W, tn), lambda bi, ti, j: (0, j)),
            tail,
            *extra_specs,
        ],
        out_specs=(attn_act, kf_act, act, act, attn_act, act, act, act, tails),
        scratch_shapes=[pltpu.VMEM((bt * ts, d), BF16),
                        pltpu.VMEM((nj, bt, CONV_W - 1, tn), F32)],
        input_output_aliases=aliases,
        compiler_params=_cparams(3),
        name=name,
    )(x, ada, *([w_in] * N_SEG), conv_w, conv_past, *extra_in)


def _stack_q(q):
    lane = lax.broadcasted_iota(jnp.int32, q.shape, 1)
    zero = jnp.zeros_like(q)
    return jnp.concatenate([jnp.where(lane < HEAD_DIM, q, zero),
                            jnp.where(lane >= HEAD_DIM, q, zero)], axis=0)


def _lam(lq1, lk1, lq2, lk2, lam_init):
    s1 = jnp.sum(lq1[...] * lk1[...], axis=-1, keepdims=True)
    s2 = jnp.sum(lq2[...] * lk2[...], axis=-1, keepdims=True)
    return jnp.exp(s1) - jnp.exp(s2) + lam_init


def _finish_heads(o1, o2, lam, g, lam_init):
    o = o1 - lam * o2
    o = o * lax.rsqrt(jnp.mean(o * o, axis=-1, keepdims=True) + LN_EPS)
    return o * g * (1.0 - lam_init)


def _pattn_kernel(qt_ref, k_ref, vt_ref, vis_a, vis_b, lq1, lk1, lq2, lk2, g_ref, o_ref,
                  qs_scr, acc_scr, sa_scr, sb_scr, *, tq, tk, lam_init):
    half = tq // 2
    nq = qt_ref.shape[1] // tq
    ones = jnp.ones((ONES_ROWS, tk), BF16)
    lam = _lam(lq1, lk1, lq2, lk2, lam_init)
    late = lambda a: jnp.concatenate([a[:, half:tq], a[:, tq + half:]], axis=1)

    def key_start(j):
        return j * tk if isinstance(j, int) else pl.multiple_of(j * tk, tk)

    def stack_queries(qi):
        qt = qt_ref[:, qi * tq:(qi + 1) * tq]
        feat = lax.broadcasted_iota(jnp.int32, qt.shape, 0)
        zero = jnp.zeros_like(qt)
        qs_scr[:, :tq] = jnp.where(feat < HEAD_DIM, qt, zero)
        qs_scr[:, tq:] = jnp.where(feat >= HEAD_DIM, qt, zero)

    def scores(j, dst):
        dst[:, :2 * tq] = _dot(k_ref[pl.ds(key_start(j), tk), :], qs_scr[...])

    def softmax_pv(j, s, m_old):
        m_new = jnp.maximum(m_old, jnp.max(s, axis=0, keepdims=True))
        a = jnp.exp2(m_old - m_new)
        p = jnp.exp2(s - m_new).astype(BF16)
        vt1 = jnp.concatenate([vt_ref[:, pl.ds(key_start(j), tk)], ones], axis=0)
        return m_new, a, _dot(vt1, p)

    def update(j, s, m_old):
        m_new, a, pv = softmax_pv(j, s, m_old)
        acc_scr[...] = a * acc_scr[...] + pv
        return m_new

    def body(i, m):
        j = 2 * i
        scores(j + 1, sb_scr)
        m = update(j, sa_scr[:, :2 * tq], m)
        scores(j + 2, sa_scr)
        return update(j + 1, sb_scr[:, :2 * tq], m)

    stack_queries(0)
    scores(0, sa_scr)
    for qi in range(nq):
        acc_scr[...] = jnp.zeros(acc_scr.shape, F32)
        m = jnp.full((1, 2 * tq), NEG_INF, F32)
        npair = (qi * tq) // (2 * tk)
        if npair:
            m = lax.fori_loop(0, npair, body, m)
        j = 2 * npair
        sb_scr[:, :tq] = _dot(k_ref[pl.ds(key_start(j + 1), tk), :], late(qs_scr[...]))
        m = update(j, jnp.where(vis_a[...] > 0.5, sa_scr[:, :2 * tq], NEG_INF), m)
        _, a, pv = softmax_pv(j + 1, jnp.where(vis_b[...] > 0.5, sb_scr[:, :tq], NEG_INF), late(m))
        acc_scr[:, half:tq] = a[:, :half] * acc_scr[:, half:tq] + pv[:, :half]
        acc_scr[:, tq + half:] = a[:, half:] * acc_scr[:, tq + half:] + pv[:, half:]
        o = acc_scr[:V_DIM, :] / acc_scr[V_DIM:V_DIM + 1, :]
        if qi + 1 < nq:
            stack_queries(qi + 1)
            scores(0, sa_scr)
        o = o[:, :tq] - lam * o[:, tq:]
        o = o * lax.rsqrt(jnp.mean(o * o, axis=0, keepdims=True) + LN_EPS)
        o_ref[qi * tq:(qi + 1) * tq, :] = (o.T * g_ref[...] * (1.0 - lam_init)).astype(o_ref.dtype)


def _pattn_call(qt, k, vt, lq1, lk1, lq2, lk2, g, *, tq, lam_init, name):
    b, t, w = k.shape
    nh = w // HEAD_W
    tk = tq // 2
    assert tk % CHUNK == 0
    kchunk = lax.broadcasted_iota(jnp.int32, (tk, 2 * tq), 0) // CHUNK
    col = lax.broadcasted_iota(jnp.int32, (tk, 2 * tq), 1)
    vis_a = (kchunk <= (col % tq) // CHUNK).astype(F32)
    vis_b = (kchunk[:, :tq] <= (col[:, :tq] % tk) // CHUNK).astype(F32)
    vec = lambda n: pl.BlockSpec((1, n), lambda bi, h: (0, 0))
    feature_major = pl.BlockSpec((None, HEAD_W, t), lambda bi, h: (bi, h, 0))
    token_major = pl.BlockSpec((None, t, HEAD_W), lambda bi, h: (bi, 0, h))
    return pl.pallas_call(
        functools.partial(_pattn_kernel, tq=tq, tk=tk, lam_init=lam_init),
        out_shape=jax.ShapeDtypeStruct((b, t, w), BF16),
        grid=(b, nh),
        in_specs=[
            feature_major, token_major, feature_major,
            pl.BlockSpec((tk, 2 * tq), lambda bi, h: (0, 0)),
            pl.BlockSpec((tk, tq), lambda bi, h: (0, 0)),
            vec(HEAD_DIM), vec(HEAD_DIM), vec(HEAD_DIM), vec(HEAD_DIM), vec(V_DIM),
        ],
        out_specs=token_major,
        scratch_shapes=[pltpu.VMEM((HEAD_W, 2 * tq), BF16),
                        pltpu.VMEM((V_DIM + ONES_ROWS, 2 * tq), F32),
                        pltpu.VMEM((tk, 2 * tq + SCORE_PAD), F32),
                        pltpu.VMEM((tk, 2 * tq + SCORE_PAD), F32)],
        compiler_params=_cparams(2),
        name=name,
    )(qt, k, vt, vis_a, vis_b, lq1, lk1, lq2, lk2, g)


def _sattn_kernel(q_ref, kt_ref, vc_ref, kn_ref, vn_ref, lq1, lk1, lq2, lk2, g_ref, o_ref,
                  qs_scr, m_scr, l_scr, acc_scr, s_scr, *, lam_init):
    c = pl.program_id(1)
    ts = q_ref.shape[0]
    nh = acc_scr.shape[0]

    def hs(h):
        return slice(h * HEAD_W, (h + 1) * HEAD_W)

    @pl.when(c == 0)
    def _():
        for h in range(nh):
            qs = _stack_q(q_ref[:, hs(h)])
            qs_scr[h] = qs
            s = _dot_nt(qs, kn_ref[:, hs(h)])
            m = jnp.max(s, axis=-1, keepdims=True)
            p = jnp.exp2(s - m)
            m_scr[h] = m
            l_scr[h] = jnp.sum(p, axis=-1, keepdims=True)
            acc_scr[h] = _dot(p.astype(BF16), vn_ref[:, hs(h)])

    pc = vc_ref.shape[0] // nh

    def scores(h):
        s_scr[h % 2] = _dot(qs_scr[h], kt_ref[hs(h), :].astype(BF16))

    scores(0)
    for h in range(nh):
        if h + 1 < nh:
            scores(h + 1)
        s = s_scr[h % 2]
        m_old = m_scr[h]
        m_new = jnp.maximum(m_old, jnp.max(s, axis=-1, keepdims=True))
        a = jnp.exp2(m_old - m_new)
        p = jnp.exp2(s - m_new)
        m_scr[h] = m_new
        l_scr[h] = a * l_scr[h] + jnp.sum(p, axis=-1, keepdims=True)
        v_h = vc_ref[pl.ds(h, pc, stride=nh), :].astype(BF16)
        acc_scr[h] = a * acc_scr[h] + _dot(p.astype(BF16), v_h)

    @pl.when(c == pl.num_programs(1) - 1)
    def _():
        lam = _lam(lq1, lk1, lq2, lk2, lam_init)
        for h in range(nh):
            o = acc_scr[h] / l_scr[h]
            o_ref[:, hs(h)] = _finish_heads(o[:ts], o[ts:], lam, g_ref[...], lam_init).astype(o_ref.dtype)


def _sattn_call(q, cache_kt, cache_v, k_new, v_new, lq1, lk1, lq2, lk2, g, *, layer, pc, lam_init, name):
    b, t, w = q.shape
    p = cache_kt.shape[3]
    nh = w // HEAD_W
    new = pl.BlockSpec((None, t, w), lambda bi, c: (bi, 0, 0))
    vec = lambda n: pl.BlockSpec((1, n), lambda bi, c: (0, 0))
    return pl.pallas_call(
        functools.partial(_sattn_kernel, lam_init=lam_init),
        out_shape=jax.ShapeDtypeStruct((b, t, w), BF16),
        grid=(b, p // pc),
        in_specs=[new,
                  pl.BlockSpec((None, None, w, pc), lambda bi, c: (layer, bi, 0, c)),
                  pl.BlockSpec((None, None, pc * nh, V_DIM), lambda bi, c: (layer, bi, c, 0)),
                  new, new,
                  vec(HEAD_DIM), vec(HEAD_DIM), vec(HEAD_DIM), vec(HEAD_DIM), vec(V_DIM)],
        out_specs=new,
        scratch_shapes=[pltpu.VMEM((nh, 2 * t, HEAD_W), BF16),
                        pltpu.VMEM((nh, 2 * t, 1), F32),
                        pltpu.VMEM((nh, 2 * t, 1), F32),
                        pltpu.VMEM((nh, 2 * t, V_DIM), F32),
                        pltpu.VMEM((2, 2 * t, pc), F32)],
        compiler_params=_cparams(2),
        name=name,
    )(q, cache_kt, cache_v, k_new, v_new, lq1, lk1, lq2, lk2, g)


def _mix_kernel(x_ref, ada_ref, on_ref, yb_ref, sga_ref, sgb_ref, wa_ref, wc_ref, wo_ref,
                g_ref, b_ref, out_ref, *, bt, ts, alpha):
    d = x_ref.shape[-1]
    m = bt * ts
    ya = _dot(on_ref[...].reshape(m, d), wa_ref[...])
    yb = _dot(yb_ref[...].reshape(m, d), wc_ref[...])
    merged = (sga_ref[...].reshape(m, d).astype(F32) * ya
              + sgb_ref[...].reshape(m, d).astype(F32) * yb)
    mix = _dot(merged.astype(BF16), wo_ref[...]).reshape(bt, ts, d)
    y = _ln(alpha * x_ref[...] + ada_ref[:, 2:3, :] * mix)
    out_ref[...] = y * g_ref[...].reshape(1, 1, d) + b_ref[...].reshape(1, 1, d)


def _mix_call(x, ada, on, yb, sga, sgb, wa, wc, wo, g, bln, *, bt, ts, alpha, name):
    b, t, d = x.shape
    act = pl.BlockSpec((bt, ts, d), lambda bi, ti: (bi, ti, 0))
    wsp = pl.BlockSpec((d, d), lambda bi, ti: (0, 0))
    vec = pl.BlockSpec((1, d), lambda bi, ti: (0, 0))
    return pl.pallas_call(
        functools.partial(_mix_kernel, bt=bt, ts=ts, alpha=alpha),
        out_shape=jax.ShapeDtypeStruct((b, t, d), F32),
        grid=(b // bt, t // ts),
        in_specs=[act, pl.BlockSpec((bt, 6, d), lambda bi, ti: (bi, 0, 0)),
                  act, act, act, act, wsp, wsp, wsp, vec, vec],
        out_specs=act,
        compiler_params=_cparams(2),
        name=name,
    )(x, ada, on, yb, sga, sgb, wa, wc, wo, g, bln)


def _ffn_kernel(x_ref, ada_ref, w1_ref, b1_ref, w2_ref, b2_ref, g_ref, b_ref, out_ref,
                u_scr, acc_scr, *, bt, ts, alpha):
    f = pl.program_id(2)
    d = x_ref.shape[-1]
    m = bt * ts

    @pl.when(f == 0)
    def _():
        u = _ln(x_ref[...]) * (1.0 + ada_ref[:, 4:5, :]) + ada_ref[:, 3:4, :]
        u_scr[...] = u.reshape(m, d).astype(BF16)
        acc_scr[...] = jnp.zeros(acc_scr.shape, F32)

    hid = jnp.maximum(_dot(u_scr[...], w1_ref[...]) + b1_ref[...], 0.0)
    acc_scr[...] += _dot((hid * hid).astype(BF16), w2_ref[...])

    @pl.when(f == pl.num_programs(2) - 1)
    def _():
        ff = (acc_scr[...] + b2_ref[...]).reshape(bt, ts, d)
        y = _ln(alpha * x_ref[...] + ada_ref[:, 5:6, :] * ff)
        out_ref[...] = y * g_ref[...].reshape(1, 1, d) + b_ref[...].reshape(1, 1, d)


def _ffn_call(x, ada, w1, b1, w2, b2, g, bln, *, bt, ts, tf, alpha, name):
    b, t, d = x.shape
    dff = w1.shape[1]
    act = pl.BlockSpec((bt, ts, d), lambda bi, ti, f: (bi, ti, 0))
    vec = pl.BlockSpec((1, d), lambda bi, ti, f: (0, 0))
    return pl.pallas_call(
        functools.partial(_ffn_kernel, bt=bt, ts=ts, alpha=alpha),
        out_shape=jax.ShapeDtypeStruct((b, t, d), F32),
        grid=(b // bt, t // ts, dff // tf),
        in_specs=[act, pl.BlockSpec((bt, 6, d), lambda bi, ti, f: (bi, 0, 0)),
                  pl.BlockSpec((d, tf), lambda bi, ti, f: (0, f)),
                  pl.BlockSpec((1, tf), lambda bi, ti, f: (0, f)),
                  pl.BlockSpec((tf, d), lambda bi, ti, f: (f, 0)),
                  vec, vec, vec],
        out_specs=act,
        scratch_shapes=[pltpu.VMEM((bt * ts, d), BF16), pltpu.VMEM((bt * ts, d), F32)],
        compiler_params=_cparams(3),
        name=name,
    )(x, ada, w1, b1, w2, b2, g, bln)


def _row_tiles(b, t, rows):
    if t >= rows:
        ts = rows
        while t % ts:
            ts //= 2
        return 1, ts
    bt = max(1, min(b, rows // t))
    while b % bt:
        bt -= 1
    return bt, t


def _div_tile(n, want):
    tile = min(n, want)
    while n % tile:
        tile //= 2
    return tile


def kernel(x_prompt, x_sample, c_prompt, c_sample, cache_k, cache_v, state_conv, w_in, lam_q1, lam_k1, lam_q2, lam_k2, subln_g, w_attn_out, conv_w, w_conv_out, w_out, w_ada, b_ada, ln1_g, ln1_b, ln2_g, ln2_b, w_ff1, b_ff1, w_ff2, b_ff2):
    n_layers, d, _ = w_in.shape
    bp, tp, _ = x_prompt.shape
    bs, tsamp, _ = x_sample.shape
    past = cache_k.shape[2]
    w = N_HEADS * HEAD_W
    alpha = (2.0 * n_layers) ** 0.25

    w_in_b = w_in.astype(BF16)
    w_ao_b = w_attn_out.astype(BF16)
    w_co_b = w_conv_out.astype(BF16)
    w_o_b = w_out.astype(BF16)
    w_ada_b = w_ada.astype(BF16)
    w1_b = w_ff1.astype(BF16)
    w2_b = w_ff2.astype(BF16)

    c_all = jnp.concatenate([c_prompt, c_sample], axis=0)
    ada_all = _ada_call(c_all, w_ada_b, b_ada[:, None, :]).reshape(n_layers, bp + bs, 6, d)

    cache_kt = jnp.transpose(cache_k, (0, 1, 3, 4, 5, 2)).reshape(n_layers, bs, w, past)
    cache_v2 = cache_v.reshape(n_layers, bs, past * N_HEADS, V_DIM)
    pc = _div_tile(past, 1024)
    zero_past = jnp.zeros((bp, CONV_W - 1, d), F32)

    bt_p, ts_p = _row_tiles(bp, tp, 1024)
    bt_s, ts_s = _row_tiles(bs, tsamp, 1024)
    mbt_p, mts_p = _row_tiles(bp, tp, 512)
    mbt_s, mts_s = _row_tiles(bs, tsamp, 512)
    tn = _div_tile(w, 256)
    tf = _div_tile(w_ff1.shape[2], 1024)
    tq = _div_tile(tp, 1024)

    xp, xs = x_prompt, x_sample
    k_stack = lax.empty((n_layers, bp, w, tp), F32)
    outs = [[] for _ in range(6)]
    for layer in range(n_layers):
        lam_init = 0.8 - 0.6 * math.exp(-0.3 * layer)
        ada_p = ada_all[layer, :bp]
        ada_s = ada_all[layer, bp:]
        lams = (lam_q1[layer][None], lam_k1[layer][None], lam_q2[layer][None], lam_k2[layer][None],
                subln_g[layer][None])
        ln1 = (ln1_g[layer][None], ln1_b[layer][None])
        ffn_w = (w1_b[layer], b_ff1[layer][None], w2_b[layer], b_ff2[layer][None],
                 ln2_g[layer][None], ln2_b[layer][None])

        q, k_stack, kb, vf, vb, yb, sga, sgb, nc = _proj_call(
            xp, ada_p, w_in_b[layer], conv_w[layer], zero_past,
            bt=bt_p, ts=ts_p, tn=tn, k_stack=k_stack, layer=layer, name=f"proj_p{layer}")
        on = _pattn_call(q, kb, vb, *lams, tq=tq, lam_init=lam_init, name=f"attn_p{layer}")
        xp = _mix_call(xp, ada_p, on, yb, sga, sgb, w_ao_b[layer], w_co_b[layer], w_o_b[layer], *ln1,
                       bt=mbt_p, ts=mts_p, alpha=alpha, name=f"mix_p{layer}")
        xp = _ffn_call(xp, ada_p, *ffn_w, bt=bt_p, ts=ts_p, tf=tf, alpha=alpha, name=f"ffn_p{layer}")
        outs[1].append(vf)
        outs[2].append(nc[:, -1])

        q, kf, kb, vf, vb, yb, sga, sgb, nc = _proj_call(
            xs, ada_s, w_in_b[layer], conv_w[layer], state_conv[layer],
            bt=bt_s, ts=ts_s, tn=tn, name=f"proj_s{layer}")
        on = _sattn_call(q, cache_kt, cache_v2, kb, vb, *lams, layer=layer, pc=pc, lam_init=lam_init,
                         name=f"attn_s{layer}")
        xs = _mix_call(xs, ada_s, on, yb, sga, sgb, w_ao_b[layer], w_co_b[layer], w_o_b[layer], *ln1,
                       bt=mbt_s, ts=mts_s, alpha=alpha, name=f"mix_s{layer}")
        xs = _ffn_call(xs, ada_s, *ffn_w, bt=bt_s, ts=ts_s, tf=tf, alpha=alpha, name=f"ffn_s{layer}")
        outs[3].append(kf)
        outs[4].append(vf)
        outs[5].append(nc[:, -1])

    new_k_p = jnp.transpose(k_stack.reshape(n_layers, bp, N_HEADS, 2, HEAD_DIM, tp), (0, 1, 5, 2, 3, 4))
    new_v_p = jnp.stack(outs[1]).reshape(n_layers, bp, tp, N_HEADS, V_DIM)
    new_c_p = jnp.stack(outs[2])
    new_k_s = jnp.stack(outs[3]).reshape(n_layers, bs, tsamp, N_HEADS, 2, HEAD_DIM)
    new_v_s = jnp.stack(outs[4]).reshape(n_layers, bs, tsamp, N_HEADS, V_DIM)
    new_c_s = jnp.stack(outs[5])
    return (xp, xs, new_k_p, new_v_p, new_c_p, new_k_s, new_v_s, new_c_s)
```

```python
import functools
import math

import jax
import jax.numpy as jnp
from jax import lax
from jax.experimental import pallas as pl
from jax.experimental.pallas import tpu as pltpu

N_HEADS = 8
HEAD_DIM = 64
V_DIM = 2 * HEAD_DIM
HEAD_W = 2 * HEAD_DIM
CHUNK = 64
CONV_W = 3
LN_EPS = 1e-5
NEG_INF = -1e30
N_SEG = 8
Q_SCALE = (HEAD_DIM ** -0.5) * math.log2(math.e)
ONES_ROWS = 16
SCORE_PAD = 128

F32 = jnp.float32
BF16 = jnp.bfloat16

VMEM_LIMIT = 56 * 1024 * 1024


def _cparams(n_axes):
    return pltpu.CompilerParams(dimension_semantics=("arbitrary",) * n_axes,
                                vmem_limit_bytes=VMEM_LIMIT)


def _ln(x):
    mu = jnp.mean(x, axis=-1, keepdims=True)
    xc = x - mu
    var = jnp.mean(xc * xc, axis=-1, keepdims=True)
    return xc * lax.rsqrt(var + LN_EPS)


def _dot(a, b):
    return jnp.dot(a, b, preferred_element_type=F32)


def _dot_nt(a, b):
    return lax.dot_general(a, b, (((1,), (1,)), ((), ())), preferred_element_type=F32)


def _ada_kernel(c_ref, w_ref, b_ref, o_ref):
    c = c_ref[...]
    sc = c * jax.nn.sigmoid(c)
    o_ref[...] = _dot(sc.astype(BF16), w_ref[...]) + b_ref[...]


def _ada_call(c_all, w_ada, b_ada):
    n_layers, d, d6 = w_ada.shape
    nb = c_all.shape[0]
    tn = d
    return pl.pallas_call(
        _ada_kernel,
        out_shape=jax.ShapeDtypeStruct((n_layers, nb, d6), F32),
        grid=(n_layers, d6 // tn),
        in_specs=[
            pl.BlockSpec((nb, d), lambda l, j: (0, 0)),
            pl.BlockSpec((None, d, tn), lambda l, j: (l, 0, j)),
            pl.BlockSpec((None, 1, tn), lambda l, j: (l, 0, j)),
        ],
        out_specs=pl.BlockSpec((None, nb, tn), lambda l, j: (l, 0, j)),
        compiler_params=_cparams(2),
        name="ada",
    )(c_all, w_ada, b_ada)


def _proj_kernel(x_ref, ada_ref, wq, wk, wv, wgb, wgc, wh, wga, wgg, cw_ref, past_ref, *rest,
                 bt, ts, feature_major):
    if feature_major:
        rest = rest[1:]
    (q_ref, kf_ref, kb_ref, vf_ref, vb_ref, yb_ref, sga_ref, sgb_ref, nc_ref, u_scr, carry_scr) = rest
    t = pl.program_id(1)
    j = pl.program_id(2)
    d = x_ref.shape[-1]
    tn = wq.shape[-1]
    m = bt * ts

    def attn_layout(a):
        if feature_major:
            return a.T.astype(BF16).reshape(bt, tn, ts)
        return a.astype(BF16).reshape(bt, ts, tn)

    @pl.when(j == 0)
    def _():
        xn = _ln(x_ref[...])
        u = xn * (1.0 + ada_ref[:, 1:2, :]) + ada_ref[:, 0:1, :]
        u_scr[...] = u.reshape(m, d).astype(BF16)

    @pl.when(t == 0)
    def _():
        carry_scr[j] = past_ref[...]

    u = u_scr[...]

    def seg(w_ref):
        return _dot(u, w_ref[...])

    q_ref[...] = attn_layout(seg(wq) * Q_SCALE)
    k = seg(wk)
    kf_ref[...] = k.T.reshape(bt, tn, ts) if feature_major else k.reshape(bt, ts, tn)
    kb_ref[...] = k.astype(BF16).reshape(bt, ts, tn)
    v = seg(wv)
    vf_ref[...] = v.reshape(bt, ts, tn)
    vb_ref[...] = attn_layout(v)

    z2 = seg(wgc) * seg(wh)
    z = z2.reshape(bt, ts, tn)
    prev = carry_scr[j]
    p0 = prev[:, 0:1, :]
    p1 = prev[:, 1:2, :]
    row = lax.broadcasted_iota(jnp.int32, (bt, ts, tn), 1)
    zr1 = pltpu.roll(z2, 1, 0).reshape(bt, ts, tn)
    zr2 = pltpu.roll(z2, 2, 0).reshape(bt, ts, tn)
    s1 = jnp.where(row == 0, p1, zr1)
    s2 = jnp.where(row == 0, p0, jnp.where(row == 1, p1, zr2))
    cw = cw_ref[...]
    conv = cw[0:1, :].reshape(1, 1, tn) * s2 + cw[1:2, :].reshape(1, 1, tn) * s1
    conv = conv + cw[2:3, :].reshape(1, 1, tn) * z
    tail = z[:, ts - (CONV_W - 1):, :]
    carry_scr[j] = tail
    nc_ref[...] = tail

    yb_ref[...] = (seg(wgb).reshape(bt, ts, tn) * conv).astype(BF16)
    sga_ref[...] = jax.nn.sigmoid(seg(wga)).astype(BF16).reshape(bt, ts, tn)
    sgb_ref[...] = jax.nn.sigmoid(seg(wgg)).astype(BF16).reshape(bt, ts, tn)


def _proj_call(x, ada, w_in, conv_w, conv_past, *, bt, ts, tn, name, k_stack=None, layer=0):
    b, t, d = x.shape
    w = w_in.shape[1] // N_SEG
    nj = w // tn
    grid = (b // bt, t // ts, nj)
    feature_major = k_stack is not None

    def wspec(s):
        return pl.BlockSpec((d, tn), lambda bi, ti, j, s=s: (0, s * nj + j))

    act = pl.BlockSpec((bt, ts, tn), lambda bi, ti, j: (bi, ti, j))
    sds = jax.ShapeDtypeStruct
    extra_in, extra_specs, aliases = (), (), {}
    if feature_major:
        attn_act = pl.BlockSpec((bt, tn, ts), lambda bi, ti, j: (bi, j, ti))
        attn_shape = sds((b, w, t), BF16)
        kf_act = pl.BlockSpec((None, bt, tn, ts), lambda bi, ti, j: (layer, bi, j, ti))
        kf_shape = sds(k_stack.shape, F32)
        extra_in, extra_specs = (k_stack,), (pl.BlockSpec(memory_space=pl.ANY),)
        aliases = {4 + N_SEG: 1}
    else:
        attn_act, attn_shape = act, sds((b, t, w), BF16)
        kf_act, kf_shape = act, sds((b, t, w), F32)
    tail = pl.BlockSpec((bt, CONV_W - 1, tn), lambda bi, ti, j: (bi, 0, j))
    tails = pl.BlockSpec((bt, None, CONV_W - 1, tn), lambda bi, ti, j: (bi, ti, 0, j))
    out_shape = (attn_shape, kf_shape, sds((b, t, w), BF16),
                 sds((b, t, w), F32), attn_shape, sds((b, t, w), BF16),
                 sds((b, t, w), BF16), sds((b, t, w), BF16), sds((b, t // ts, CONV_W - 1, w), F32))
    return pl.pallas_call(
        functools.partial(_proj_kernel, bt=bt, ts=ts, feature_major=feature_major),
        out_shape=out_shape,
        grid=grid,
        in_specs=[
            pl.BlockSpec((bt, ts, d), lambda bi, ti, j: (bi, ti, 0)),
            pl.BlockSpec((bt, 6, d), lambda bi, ti, j: (bi, 0, 0)),
            *[wspec(s) for s in range(N_SEG)],
            pl.BlockSpec((CONV_W, tn), lambda bi, ti, j: (0, j)),
            tail,
            *extra_specs,
        ],
        out_specs=(attn_act, kf_act, act, act, attn_act, act, act, act, tails),
        scratch_shapes=[pltpu.VMEM((bt * ts, d), BF16),
                        pltpu.VMEM((nj, bt, CONV_W - 1, tn), F32)],
        input_output_aliases=aliases,
        compiler_params=_cparams(3),
        name=name,
    )(x, ada, *([w_in] * N_SEG), conv_w, conv_past, *extra_in)


def _stack_q(q):
    lane = lax.broadcasted_iota(jnp.int32, q.shape, 1)
    zero = jnp.zeros_like(q)
    return jnp.concatenate([jnp.where(lane < HEAD_DIM, q, zero),
                            jnp.where(lane >= HEAD_DIM, q, zero)], axis=0)


def _lam(lq1, lk1, lq2, lk2, lam_init):
    s1 = jnp.sum(lq1[...] * lk1[...], axis=-1, keepdims=True)
    s2 = jnp.sum(lq2[...] * lk2[...], axis=-1, keepdims=True)
    return jnp.exp(s1) - jnp.exp(s2) + lam_init


def _finish_heads(o1, o2, lam, g, lam_init):
    o = o1 - lam * o2
    o = o * lax.rsqrt(jnp.mean(o * o, axis=-1, keepdims=True) + LN_EPS)
    return o * g * (1.0 - lam_init)


def _pattn_kernel(qt_ref, k_ref, vt_ref, vis_a, vis_b, lq1, lk1, lq2, lk2, g_ref, o_ref,
                  qs_scr, acc_scr, sa_scr, sb_scr, *, tq, tk, lam_init):
    half = tq // 2
    nq = qt_ref.shape[1] // tq
    ones = jnp.ones((ONES_ROWS, tk), BF16)
    lam = _lam(lq1, lk1, lq2, lk2, lam_init)
    late = lambda a: jnp.concatenate([a[:, half:tq], a[:, tq + half:]], axis=1)

    def key_start(j):
        return j * tk if isinstance(j, int) else pl.multiple_of(j * tk, tk)

    def stack_queries(qi):
        qt = qt_ref[:, qi * tq:(qi + 1) * tq]
        feat = lax.broadcasted_iota(jnp.int32, qt.shape, 0)
        zero = jnp.zeros_like(qt)
        qs_scr[:, :tq] = jnp.where(feat < HEAD_DIM, qt, zero)
        qs_scr[:, tq:] = jnp.where(feat >= HEAD_DIM, qt, zero)

    def scores(j, dst):
        dst[:, :2 * tq] = _dot(k_ref[pl.ds(key_start(j), tk), :], qs_scr[...])

    def softmax_pv(j, s, m_old):
        m_new = jnp.maximum(m_old, jnp.max(s, axis=0, keepdims=True))
        a = jnp.exp2(m_old - m_new)
        p = jnp.exp2(s - m_new).astype(BF16)
        vt1 = jnp.concatenate([vt_ref[:, pl.ds(key_start(j), tk)], ones], axis=0)
        return m_new, a, _dot(vt1, p)

    def update(j, s, m_old):
        m_new, a, pv = softmax_pv(j, s, m_old)
        acc_scr[...] = a * acc_scr[...] + pv
        return m_new

    def body(i, m):
        j = 2 * i
        scores(j + 1, sb_scr)
        m = update(j, sa_scr[:, :2 * tq], m)
        scores(j + 2, sa_scr)
        return update(j + 1, sb_scr[:, :2 * tq], m)

    stack_queries(0)
    scores(0, sa_scr)
    for qi in range(nq):
        acc_scr[...] = jnp.zeros(acc_scr.shape, F32)
        m = jnp.full((1, 2 * tq), NEG_INF, F32)
        npair = (qi * tq) // (2 * tk)
        if npair:
            m = lax.fori_loop(0, npair, body, m)
        j = 2 * npair
        sb_scr[:, :tq] = _dot(k_ref[pl.ds(key_start(j + 1), tk), :], late(qs_scr[...]))
        m = update(j, jnp.where(vis_a[...] > 0.5, sa_scr[:, :2 * tq], NEG_INF), m)
        _, a, pv = softmax_pv(j + 1, jnp.where(vis_b[...] > 0.5, sb_scr[:, :tq], NEG_INF), late(m))
        acc_scr[:, half:tq] = a[:, :half] * acc_scr[:, half:tq] + pv[:, :half]
        acc_scr[:, tq + half:] = a[:, half:] * acc_scr[:, tq + half:] + pv[:, half:]
        o = acc_scr[:V_DIM, :] / acc_scr[V_DIM:V_DIM + 1, :]
        if qi + 1 < nq:
            stack_queries(qi + 1)
            scores(0, sa_scr)
        o = o[:, :tq] - lam * o[:, tq:]
        o = o * lax.rsqrt(jnp.mean(o * o, axis=0, keepdims=True) + LN_EPS)
        o_ref[qi * tq:(qi + 1) * tq, :] = (o.T * g_ref[...] * (1.0 - lam_init)).astype(o_ref.dtype)


def _pattn_call(qt, k, vt, lq1, lk1, lq2, lk2, g, *, tq, lam_init, name):
    b, t, w = k.shape
    nh = w // HEAD_W
    tk = tq // 2
    assert tk % CHUNK == 0
    kchunk = lax.broadcasted_iota(jnp.int32, (tk, 2 * tq), 0) // CHUNK
    col = lax.broadcasted_iota(jnp.int32, (tk, 2 * tq), 1)
    vis_a = (kchunk <= (col % tq) // CHUNK).astype(F32)
    vis_b = (kchunk[:, :tq] <= (col[:, :tq] % tk) // CHUNK).astype(F32)
    vec = lambda n: pl.BlockSpec((1, n), lambda bi, h: (0, 0))
    feature_major = pl.BlockSpec((None, HEAD_W, t), lambda bi, h: (bi, h, 0))
    token_major = pl.BlockSpec((None, t, HEAD_W), lambda bi, h: (bi, 0, h))
    return pl.pallas_call(
        functools.partial(_pattn_kernel, tq=tq, tk=tk, lam_init=lam_init),
        out_shape=jax.ShapeDtypeStruct((b, t, w), BF16),
        grid=(b, nh),
        in_specs=[
            feature_major, token_major, feature_major,
            pl.BlockSpec((tk, 2 * tq), lambda bi, h: (0, 0)),
            pl.BlockSpec((tk, tq), lambda bi, h: (0, 0)),
            vec(HEAD_DIM), vec(HEAD_DIM), vec(HEAD_DIM), vec(HEAD_DIM), vec(V_DIM),
        ],
        out_specs=token_major,
        scratch_shapes=[pltpu.VMEM((HEAD_W, 2 * tq), BF16),
                        pltpu.VMEM((V_DIM + ONES_ROWS, 2 * tq), F32),
                        pltpu.VMEM((tk, 2 * tq + SCORE_PAD), F32),
                        pltpu.VMEM((tk, 2 * tq + SCORE_PAD), F32)],
        compiler_params=_cparams(2),
        name=name,
    )(qt, k, vt, vis_a, vis_b, lq1, lk1, lq2, lk2, g)


def _sattn_kernel(q_ref, kt_ref, vc_ref, kn_ref, vn_ref, lq1, lk1, lq2, lk2, g_ref, o_ref,
                  qs_scr, m_scr, l_scr, acc_scr, s_scr, *, lam_init):
    c = pl.program_id(1)
    ts = q_ref.shape[0]
    nh = acc_scr.shape[0]

    def hs(h):
        return slice(h * HEAD_W, (h + 1) * HEAD_W)

    @pl.when(c == 0)
    def _():
        for h in range(nh):
            qs = _stack_q(q_ref[:, hs(h)])
            qs_scr[h] = qs
            s = _dot_nt(qs, kn_ref[:, hs(h)])
            m = jnp.max(s, axis=-1, keepdims=True)
            p = jnp.exp2(s - m)
            m_scr[h] = m
            l_scr[h] = jnp.sum(p, axis=-1, keepdims=True)
            acc_scr[h] = _dot(p.astype(BF16), vn_ref[:, hs(h)])

    pc = vc_ref.shape[0] // nh

    def scores(h):
        s_scr[h % 2] = _dot(qs_scr[h], kt_ref[hs(h), :].astype(BF16))

    scores(0)
    for h in range(nh):
        if h + 1 < nh:
            scores(h + 1)
        s = s_scr[h % 2]
        m_old = m_scr[h]
        m_new = jnp.maximum(m_old, jnp.max(s, axis=-1, keepdims=True))
        a = jnp.exp2(m_old - m_new)
        p = jnp.exp2(s - m_new)
        m_scr[h] = m_new
        l_scr[h] = a * l_scr[h] + jnp.sum(p, axis=-1, keepdims=True)
        v_h = vc_ref[pl.ds(h, pc, stride=nh), :].astype(BF16)
        acc_scr[h] = a * acc_scr[h] + _dot(p.astype(BF16), v_h)

    @pl.when(c == pl.num_programs(1) - 1)
    def _():
        lam = _lam(lq1, lk1, lq2, lk2, lam_init)
        for h in range(nh):
            o = acc_scr[h] / l_scr[h]
            o_ref[:, hs(h)] = _finish_heads(o[:ts], o[ts:], lam, g_ref[...], lam_init).astype(o_ref.dtype)


def _sattn_call(q, cache_kt, cache_v, k_new, v_new, lq1, lk1, lq2, lk2, g, *, layer, pc, lam_init, name):
    b, t, w = q.shape
    p = cache_kt.shape[3]
    nh = w // HEAD_W
    new = pl.BlockSpec((None, t, w), lambda bi, c: (bi, 0, 0))
    vec = lambda n: pl.BlockSpec((1, n), lambda bi, c: (0, 0))
    return pl.pallas_call(
        functools.partial(_sattn_kernel, lam_init=lam_init),
        out_shape=jax.ShapeDtypeStruct((b, t, w), BF16),
        grid=(b, p // pc),
        in_specs=[new,
                  pl.BlockSpec((None, None, w, pc), lambda bi, c: (layer, bi, 0, c)),
                  pl.BlockSpec((None, None, pc * nh, V_DIM), lambda bi, c: (layer, bi, c, 0)),
                  new, new,
                  vec(HEAD_DIM), vec(HEAD_DIM), vec(HEAD_DIM), vec(HEAD_DIM), vec(V_DIM)],
        out_specs=new,
        scratch_shapes=[pltpu.VMEM((nh, 2 * t, HEAD_W), BF16),
                        pltpu.VMEM((nh, 2 * t, 1), F32),
                        pltpu.VMEM((nh, 2 * t, 1), F32),
                        pltpu.VMEM((nh, 2 * t, V_DIM), F32),
                        pltpu.VMEM((2, 2 * t, pc), F32)],
        compiler_params=_cparams(2),
        name=name,
    )(q, cache_kt, cache_v, k_new, v_new, lq1, lk1, lq2, lk2, g)


def _mix_kernel(x_ref, ada_ref, on_ref, yb_ref, sga_ref, sgb_ref, wa_ref, wc_ref, wo_ref,
                g_ref, b_ref, out_ref, *, bt, ts, alpha):
    d = x_ref.shape[-1]
    m = bt * ts
    ya = _dot(on_ref[...].reshape(m, d), wa_ref[...])
    yb = _dot(yb_ref[...].reshape(m, d), wc_ref[...])
    merged = (sga_ref[...].reshape(m, d).astype(F32) * ya
              + sgb_ref[...].reshape(m, d).astype(F32) * yb)
    mix = _dot(merged.astype(BF16), wo_ref[...]).reshape(bt, ts, d)
    y = _ln(alpha * x_ref[...] + ada_ref[:, 2:3, :] * mix)
    out_ref[...] = y * g_ref[...].reshape(1, 1, d) + b_ref[...].reshape(1, 1, d)


def _mix_call(x, ada, on, yb, sga, sgb, wa, wc, wo, g, bln, *, bt, ts, alpha, name):
    b, t, d = x.shape
    act = pl.BlockSpec((bt, ts, d), lambda bi, ti: (bi, ti, 0))
    wsp = pl.BlockSpec((d, d), lambda bi, ti: (0, 0))
    vec = pl.BlockSpec((1, d), lambda bi, ti: (0, 0))
    return pl.pallas_call(
        functools.partial(_mix_kernel, bt=bt, ts=ts, alpha=alpha),
        out_shape=jax.ShapeDtypeStruct((b, t, d), F32),
        grid=(b // bt, t // ts),
        in_specs=[act, pl.BlockSpec((bt, 6, d), lambda bi, ti: (bi, 0, 0)),
                  act, act, act, act, wsp, wsp, wsp, vec, vec],
        out_specs=act,
        compiler_params=_cparams(2),
        name=name,
    )(x, ada, on, yb, sga, sgb, wa, wc, wo, g, bln)


def _ffn_kernel(x_ref, ada_ref, w1_ref, b1_ref, w2_ref, b2_ref, g_ref, b_ref, out_ref,
                u_scr, acc_scr, *, bt, ts, alpha):
    f = pl.program_id(2)
    d = x_ref.shape[-1]
    m = bt * ts

    @pl.when(f == 0)
    def _():
        u = _ln(x_ref[...]) * (1.0 + ada_ref[:, 4:5, :]) + ada_ref[:, 3:4, :]
        u_scr[...] = u.reshape(m, d).astype(BF16)
        acc_scr[...] = jnp.zeros(acc_scr.shape, F32)

    hid = jnp.maximum(_dot(u_scr[...], w1_ref[...]) + b1_ref[...], 0.0)
    acc_scr[...] += _dot((hid * hid).astype(BF16), w2_ref[...])

    @pl.when(f == pl.num_programs(2) - 1)
    def _():
        ff = (acc_scr[...] + b2_ref[...]).reshape(bt, ts, d)
        y = _ln(alpha * x_ref[...] + ada_ref[:, 5:6, :] * ff)
        out_ref[...] = y * g_ref[...].reshape(1, 1, d) + b_ref[...].reshape(1, 1, d)


def _ffn_call(x, ada, w1, b1, w2, b2, g, bln, *, bt, ts, tf, alpha, name):
    b, t, d = x.shape
    dff = w1.shape[1]
    act = pl.BlockSpec((bt, ts, d), lambda bi, ti, f: (bi, ti, 0))
    vec = pl.BlockSpec((1, d), lambda bi, ti, f: (0, 0))
    return pl.pallas_call(
        functools.partial(_ffn_kernel, bt=bt, ts=ts, alpha=alpha),
        out_shape=jax.ShapeDtypeStruct((b, t, d), F32),
        grid=(b // bt, t // ts, dff // tf),
        in_specs=[act, pl.BlockSpec((bt, 6, d), lambda bi, ti, f: (bi, 0, 0)),
                  pl.BlockSpec((d, tf), lambda bi, ti, f: (0, f)),
                  pl.BlockSpec((1, tf), lambda bi, ti, f: (0, f)),
                  pl.BlockSpec((tf, d), lambda bi, ti, f: (f, 0)),
                  vec, vec, vec],
        out_specs=act,
        scratch_shapes=[pltpu.VMEM((bt * ts, d), BF16), pltpu.VMEM((bt * ts, d), F32)],
        compiler_params=_cparams(3),
        name=name,
    )(x, ada, w1, b1, w2, b2, g, bln)


def _row_tiles(b, t, rows):
    if t >= rows:
        ts = rows
        while t % ts:
            ts //= 2
        return 1, ts
    bt = max(1, min(b, rows // t))
    while b % bt:
        bt -= 1
    return bt, t


def _div_tile(n, want):
    tile = min(n, want)
    while n % tile:
        tile //= 2
    return tile


def kernel(x_prompt, x_sample, c_prompt, c_sample, cache_k, cache_v, state_conv, w_in, lam_q1, lam_k1, lam_q2, lam_k2, subln_g, w_attn_out, conv_w, w_conv_out, w_out, w_ada, b_ada, ln1_g, ln1_b, ln2_g, ln2_b, w_ff1, b_ff1, w_ff2, b_ff2):
    n_layers, d, _ = w_in.shape
    bp, tp, _ = x_prompt.shape
    bs, tsamp, _ = x_sample.shape
    past = cache_k.shape[2]
    w = N_HEADS * HEAD_W
    alpha = (2.0 * n_layers) ** 0.25

    w_in_b = w_in.astype(BF16)
    w_ao_b = w_attn_out.astype(BF16)
    w_co_b = w_conv_out.astype(BF16)
    w_o_b = w_out.astype(BF16)
    w_ada_b = w_ada.astype(BF16)
    w1_b = w_ff1.astype(BF16)
    w2_b = w_ff2.astype(BF16)

    c_all = jnp.concatenate([c_prompt, c_sample], axis=0)
    ada_all = _ada_call(c_all, w_ada_b, b_ada[:, None, :]).reshape(n_layers, bp + bs, 6, d)

    cache_kt = jnp.transpose(cache_k, (0, 1, 3, 4, 5, 2)).reshape(n_layers, bs, w, past)
    cache_v2 = cache_v.reshape(n_layers, bs, past * N_HEADS, V_DIM)
    pc = _div_tile(past, 1024)
    zero_past = jnp.zeros((bp, CONV_W - 1, d), F32)

    bt_p, ts_p = _row_tiles(bp, tp, 1024)
    bt_s, ts_s = _row_tiles(bs, tsamp, 1024)
    mbt_p, mts_p = _row_tiles(bp, tp, 1024)
    mbt_s, mts_s = _row_tiles(bs, tsamp, 1024)
    tn = _div_tile(w, 256)
    tf = _div_tile(w_ff1.shape[2], 1024)
    tq = _div_tile(tp, 1024)

    xp, xs = x_prompt, x_sample
    k_stack = lax.empty((n_layers, bp, w, tp), F32)
    outs = [[] for _ in range(6)]
    for layer in range(n_layers):
        lam_init = 0.8 - 0.6 * math.exp(-0.3 * layer)
        ada_p = ada_all[layer, :bp]
        ada_s = ada_all[layer, bp:]
        lams = (lam_q1[layer][None], lam_k1[layer][None], lam_q2[layer][None], lam_k2[layer][None],
                subln_g[layer][None])
        ln1 = (ln1_g[layer][None], ln1_b[layer][None])
        ffn_w = (w1_b[layer], b_ff1[layer][None], w2_b[layer], b_ff2[layer][None],
                 ln2_g[layer][None], ln2_b[layer][None])

        q, k_stack, kb, vf, vb, yb, sga, sgb, nc = _proj_call(
            xp, ada_p, w_in_b[layer], conv_w[layer], zero_past,
            bt=bt_p, ts=ts_p, tn=tn, k_stack=k_stack, layer=layer, name=f"proj_p{layer}")
        on = _pattn_call(q, kb, vb, *lams, tq=tq, lam_init=lam_init, name=f"attn_p{layer}")
        xp = _mix_call(xp, ada_p, on, yb, sga, sgb, w_ao_b[layer], w_co_b[layer], w_o_b[layer], *ln1,
                       bt=mbt_p, ts=mts_p, alpha=alpha, name=f"mix_p{layer}")
        xp = _ffn_call(xp, ada_p, *ffn_w, bt=bt_p, ts=ts_p, tf=tf, alpha=alpha, name=f"ffn_p{layer}")
        outs[1].append(vf)
        outs[2].append(nc[:, -1])

        q, kf, kb, vf, vb, yb, sga, sgb, nc = _proj_call(
            xs, ada_s, w_in_b[layer], conv_w[layer], state_conv[layer],
            bt=bt_s, ts=ts_s, tn=tn, name=f"proj_s{layer}")
        on = _sattn_call(q, cache_kt, cache_v2, kb, vb, *lams, layer=layer, pc=pc, lam_init=lam_init,
                         name=f"attn_s{layer}")
        xs = _mix_call(xs, ada_s, on, yb, sga, sgb, w_ao_b[layer], w_co_b[layer], w_o_b[layer], *ln1,
                       bt=mbt_s, ts=mts_s, alpha=alpha, name=f"mix_s{layer}")
        xs = _ffn_call(xs, ada_s, *ffn_w, bt=bt_s, ts=ts_s, tf=tf, alpha=alpha, name=f"ffn_s{layer}")
        outs[3].append(kf)
        outs[4].append(vf)
        outs[5].append(nc[:, -1])

    new_k_p = jnp.transpose(k_stack.reshape(n_layers, bp, N_HEADS, 2, HEAD_DIM, tp), (0, 1, 5, 2, 3, 4))
    new_v_p = jnp.stack(outs[1]).reshape(n_layers, bp, tp, N_HEADS, V_DIM)
    new_c_p = jnp.stack(outs[2])
    new_k_s = jnp.stack(outs[3]).reshape(n_layers, bs, tsamp, N_HEADS, 2, HEAD_DIM)
    new_v_s = jnp.stack(outs[4]).reshape(n_layers, bs, tsamp, N_HEADS, V_DIM)
    new_c_s = jnp.stack(outs[5])
    return (xp, xs, new_k_p, new_v_p, new_c_p, new_k_s, new_v_s, new_c_s)
```
